```python
import jax, jax.numpy as jnp
from jax import lax
import numpy as np

D_MODEL = 1024
BATCH = 8
SEQ = 2048
DEPTH = 4
DEC_BATCH = 128
DEC_SEQ = 8
PAST_LEN = 16384
PAGE_SIZE = 128

A_WIDTH = D_MODEL // 2
A_HEAD = 64
A_HEADS = A_WIDTH // A_HEAD
A_W_RANK = 64
A_A_RANK = 64
A_G_RANK = 128
A_COLS = 3 * A_WIDTH + A_W_RANK + A_A_RANK + A_G_RANK
GN_EPS = 64e-5
B_WIDTH = D_MODEL // 2
POOL_WINDOWS = (2, 4, 8, 16)
B_GROUPS = len(POOL_WINDOWS)
B_GROUP = B_WIDTH // B_GROUPS
POOL_BUF = max(POOL_WINDOWS) - 1
B_COLS = B_WIDTH
C_WIDTH = D_MODEL // 2
C_HEAD = 128
C_HEADS = C_WIDTH // C_HEAD
CONV_W = 4
CHUNK = 64
C_COLS = 4 * C_WIDTH + 2 * C_HEADS
N_MEM = 256
M_HEADS = 4
M_HEAD = 64
M_WIDTH = M_HEADS * M_HEAD
M_COLS = M_WIDTH
N_BRANCH = 4
G_COLS = N_BRANCH * D_MODEL
N_IN = A_COLS + B_COLS + C_COLS + M_COLS + G_COLS
BR_WIDTH = A_WIDTH + B_WIDTH + C_WIDTH + M_WIDTH
D_FF = 2 * D_MODEL
ALPHA = (2.0 * DEPTH) ** 0.25
BETA = (8.0 * DEPTH) ** -0.25
LN_EPS = 1e-5
F32 = jnp.float32

kernel_name = 'hybrid_rwkv7_pool_gdn_memory_decoder'


def layer_norm(x, g, b):
    xf = x.astype(F32)
    mu = jnp.mean(xf, axis=-1, keepdims=True)
    var = jnp.mean(jnp.square(xf - mu), axis=-1, keepdims=True)
    return ((xf - mu) * lax.rsqrt(var + LN_EPS) * g.astype(F32) + b.astype(F32)).astype(x.dtype)


def l2_normalize(x):
    xf = x.astype(F32)
    return xf * lax.rsqrt(jnp.sum(xf * xf, axis=-1, keepdims=True) + 1e-12)


def swiglu(x, w_in, w_out):
    gate, up = jnp.split(x @ w_in, 2, axis=-1)
    return (jax.nn.silu(gate) * up) @ w_out


def rwkv7_branch(pa, shift_prev, s0, mu, w0, w_up, a0, a_up, g_up, k_k, k_a, r_k, gn_w, gn_b):
    bsz, seq, _ = pa.shape
    prev = jnp.concatenate([shift_prev[:, None, :], pa[:, :-1]], axis=1)
    xm = pa + (prev - pa) * mu
    r, k, v, xw, xa, xg = jnp.split(
        xm, [A_WIDTH, 2 * A_WIDTH, 3 * A_WIDTH, 3 * A_WIDTH + A_W_RANK, 3 * A_WIDTH + A_W_RANK + A_A_RANK], axis=-1)
    z = (w0 + jnp.tanh(xw) @ w_up).astype(F32)
    decay = jnp.exp(-jnp.exp(-jax.nn.softplus(-z) - 0.5))
    a = jax.nn.sigmoid((a0 + xa @ a_up).astype(F32))
    g = jax.nn.sigmoid(xg) @ g_up
    heads = lambda t: t.astype(F32).reshape(bsz, seq, A_HEADS, A_HEAD)
    kk = l2_normalize(heads(k * k_k))
    k = heads(k * (1.0 + (a - 1.0) * k_a))
    r, v, decay, a = heads(r), heads(v), heads(decay), heads(a)

    def step(s, inp):
        r_t, w_t, k_t, v_t, kk_t, a_t = inp
        sa = jnp.einsum('bhvk,bhk->bhv', s, kk_t)
        s = (s * w_t[:, :, None, :] - sa[..., None] * (kk_t * a_t)[:, :, None, :]
             + v_t[..., None] * k_t[:, :, None, :])
        return s, jnp.einsum('bhvk,bhk->bhv', s, r_t)

    xs = tuple(jnp.moveaxis(t, 1, 0) for t in (r, decay, k, v, kk, a))
    s_new, y = lax.scan(step, s0.astype(F32), xs)
    y = jnp.moveaxis(y, 0, 1)
    mean = jnp.mean(y, axis=-1, keepdims=True)
    var = jnp.mean(jnp.square(y - mean), axis=-1, keepdims=True)
    y = ((y - mean) * lax.rsqrt(var + GN_EPS)).reshape(bsz, seq, A_WIDTH) * gn_w + gn_b
    bonus = jnp.sum(r * k * r_k.astype(F32), axis=-1, keepdims=True) * v
    out = (y + bonus.reshape(bsz, seq, A_WIDTH)) * g
    return out.astype(pa.dtype), pa[:, -1], s_new.astype(s0.dtype)


def pool_branch(pb, buf, pool_w, pool_scale, pos0):
    bsz, seq, _ = pb.shape
    ext = jnp.concatenate([buf, pb], axis=1)
    cs = jnp.pad(jnp.cumsum(ext.astype(F32), axis=1), ((0, 0), (1, 0), (0, 0)))
    t = jnp.arange(seq)
    means = []
    for j, w in enumerate(POOL_WINDOWS):
        sl = slice(j * B_GROUP, (j + 1) * B_GROUP)
        hi = cs[:, POOL_BUF + 1:POOL_BUF + 1 + seq, sl]
        lo = cs[:, POOL_BUF + 1 - w:POOL_BUF + 1 - w + seq, sl]
        count = jnp.minimum(pos0 + t + 1, w).astype(F32)[None, :, None]
        means.append((hi - lo) / count)
    pooled = jnp.concatenate(means, axis=-1) - pb.astype(F32)
    out = jnp.einsum('blgc,gcd->blgd', pooled.reshape(bsz, seq, B_GROUPS, B_GROUP), pool_w.astype(F32))
    out = out.reshape(bsz, seq, B_WIDTH) * pool_scale
    return out.astype(pb.dtype), ext[:, -POOL_BUF:]


def gated_delta_chunked(q, k, v, beta, g, s0):
    bsz, seq, nh, dk = q.shape
    pad = (-seq) % CHUNK
    n = (seq + pad) // CHUNK

    def to_blocks(t):
        t = jnp.pad(t.astype(F32), [(0, 0), (0, pad)] + [(0, 0)] * (t.ndim - 2))
        t = t.reshape((bsz, n, CHUNK) + t.shape[2:])
        return jnp.moveaxis(t, (1, 3), (0, 2))

    qb, kb, vb, bb, gb = to_blocks(q), to_blocks(k), to_blocks(v), to_blocks(beta), to_blocks(g)
    cum = jnp.cumsum(gb, axis=-1)
    incl = jnp.tril(jnp.ones((CHUNK, CHUNK), dtype=bool))
    strict = jnp.tril(jnp.ones((CHUNK, CHUNK), dtype=bool), -1)
    diff = cum[..., :, None] - cum[..., None, :]
    decay = jnp.where(incl, jnp.exp(jnp.where(incl, diff, 0.0)), 0.0)
    k_beta = kb * bb[..., None]
    lmat = jnp.where(strict, jnp.einsum('nbhcd,nbhsd->nbhcs', k_beta, kb) * decay, 0.0)
    eye = jnp.eye(CHUNK, dtype=F32)
    tmat = lax.linalg.triangular_solve(eye + lmat, jnp.broadcast_to(eye, lmat.shape),
                                       left_side=True, lower=True, unit_diagonal=True)
    u = tmat @ (vb * bb[..., None])
    w = tmat @ (k_beta * jnp.exp(cum)[..., None])
    aqk = jnp.einsum('nbhcd,nbhsd->nbhcs', qb, kb) * decay
    g_last = cum[..., -1]
    k_dec = kb * jnp.exp(g_last[..., None] - cum)[..., None]
    q_dec = qb * jnp.exp(cum)[..., None]

    def step(s, inp):
        u_i, w_i, a_i, qd_i, kd_i, gl_i = inp
        v_new = u_i - w_i @ s
        o = qd_i @ s + a_i @ v_new
        s = s * jnp.exp(gl_i)[..., None, None] + jnp.einsum('bhcd,bhce->bhde', kd_i, v_new)
        return s, o

    s_new, o = lax.scan(step, s0, (u, w, aqk, q_dec, k_dec, g_last))
    o = jnp.moveaxis(o, (0, 2), (1, 3)).reshape(bsz, n * CHUNK, nh, vb.shape[-1])[:, :seq]
    return o, s_new


def delta_branch(pc, conv_buf, s0, conv_w, a_log, dt_bias, norm_w):
    bsz, seq, _ = pc.shape
    qkv, zg, b, a = jnp.split(pc, [3 * C_WIDTH, 4 * C_WIDTH, 4 * C_WIDTH + C_HEADS], axis=-1)
    ext = jnp.concatenate([conv_buf, qkv], axis=1)
    conv = ext[:, 0:seq] * conv_w[0]
    for j in range(1, CONV_W):
        conv = conv + ext[:, j:j + seq] * conv_w[j]
    q, k, v = jnp.split(jax.nn.silu(conv), 3, axis=-1)
    heads = lambda t: t.reshape(bsz, seq, C_HEADS, C_HEAD)
    q = l2_normalize(heads(q)) * C_HEAD ** -0.5
    k = l2_normalize(heads(k))
    v = heads(v).astype(F32)
    beta = jax.nn.sigmoid(b.astype(F32))
    g = -jnp.exp(a_log.astype(F32)) * jax.nn.softplus(a.astype(F32) + dt_bias)
    o, s_new = gated_delta_chunked(q, k, v, beta, g, s0.astype(F32))
    o = o * lax.rsqrt(jnp.mean(o * o, axis=-1, keepdims=True) + 1e-6) * norm_w
    o = o * jax.nn.silu(heads(zg).astype(F32))
    return o.reshape(bsz, seq, C_WIDTH).astype(pc.dtype), ext[:, -(CONV_W - 1):], s_new.astype(s0.dtype)


def memory_attention(pm, mem_k, mem_v):
    bsz, seq, _ = pm.shape
    q = pm.reshape(bsz, seq, M_HEADS, M_HEAD)
    s = jnp.einsum('blhd,bmhd->bhlm', q, mem_k).astype(F32) * M_HEAD ** -0.5
    p = jax.nn.softmax(s, axis=-1)
    o = jnp.einsum('bhlm,bmhd->blhd', p.astype(mem_v.dtype), mem_v)
    return o.reshape(bsz, seq, M_WIDTH)


def token_mixing(h, mem_k, mem_v, shift, s_rwkv, pool_buf, conv_buf, s_delta, p, pos0):
    bsz, seq, _ = h.shape
    proj = h @ p['w_in']
    pa, pb, pc, pm, pg = jnp.split(
        proj, [A_COLS, A_COLS + B_COLS, A_COLS + B_COLS + C_COLS, A_COLS + B_COLS + C_COLS + M_COLS], axis=-1)
    oa, shift_new, s_rwkv_new = rwkv7_branch(
        pa, shift, s_rwkv, p['rwkv_mu'], p['rwkv_w0'], p['rwkv_w_up'], p['rwkv_a0'], p['rwkv_a_up'],
        p['rwkv_g_up'], p['rwkv_k_k'], p['rwkv_k_a'], p['rwkv_r_k'], p['rwkv_gn_w'], p['rwkv_gn_b'])
    ob, pool_new = pool_branch(pb, pool_buf, p['pool_w'], p['pool_scale'], pos0)
    oc, conv_new, s_delta_new = delta_branch(
        pc, conv_buf, s_delta, p['delta_conv_w'], p['delta_a_log'], p['delta_dt_bias'], p['delta_norm_w'])
    om = memory_attention(pm, mem_k, mem_v)
    gates = jax.nn.sigmoid(pg.reshape(bsz, seq, N_BRANCH, D_MODEL).astype(F32)).astype(h.dtype)
    wb = p['w_branch']
    bounds = (0, A_WIDTH, A_WIDTH + B_WIDTH, A_WIDTH + B_WIDTH + C_WIDTH, BR_WIDTH)
    merged = None
    for i, o in enumerate((oa, ob, oc, om)):
        term = gates[:, :, i] * (o @ wb[bounds[i]:bounds[i + 1]])
        merged = term if merged is None else merged + term
    return merged @ p['w_out'], (shift_new, s_rwkv_new, pool_new, conv_new, s_delta_new)


def run_trunk(x, mem_k, mem_v, shift, s_rwkv, pool_buf, conv_buf, s_delta, params, pos0):
    new = []
    for l in range(DEPTH):
        p = params[l]
        x = layer_norm(ALPHA * x + 0.5 * swiglu(x, p['ffn1_w_in'], p['ffn1_w_out']), p['ln_g'][0], p['ln_b'][0])
        mix, st = token_mixing(x, mem_k[l], mem_v[l], shift[l], s_rwkv[l], pool_buf[l], conv_buf[l],
                               s_delta[l], p, pos0)
        x = layer_norm(ALPHA * x + mix, p['ln_g'][1], p['ln_b'][1])
        x = layer_norm(ALPHA * x + 0.5 * swiglu(x, p['ffn2_w_in'], p['ffn2_w_out']), p['ln_g'][2], p['ln_b'][2])
        new.append(st)
    shift_n, rwkv_n, pool_n, conv_n, delta_n = [jnp.stack([st[i] for st in new]) for i in range(5)]
    return x, shift_n, rwkv_n, pool_n, conv_n, delta_n


def setup_inputs(seed: int = 0) -> dict:
    key = jax.random.key(seed)
    keys = iter(jax.random.split(key, 64))

    def nrm(shape, scale):
        return scale * jax.random.normal(next(keys), shape, F32)

    def unif(shape, lo, hi):
        return jax.random.uniform(next(keys), shape, F32, lo, hi)

    L = DEPTH
    return {
        'x_prompt': nrm((BATCH, SEQ, D_MODEL), 1.0),
        'x_sample': nrm((DEC_BATCH, DEC_SEQ, D_MODEL), 1.0),
        'mem_prompt': nrm((BATCH, N_MEM, D_MODEL), 1.0),
        'cache_mem_k': nrm((L, DEC_BATCH, N_MEM, M_HEADS, M_HEAD), 1.0),
        'cache_mem_v': nrm((L, DEC_BATCH, N_MEM, M_HEADS, M_HEAD), 1.0),
        'state_rwkv': nrm((L, DEC_BATCH, A_HEADS, A_HEAD, A_HEAD), 0.1),
        'state_rwkv_shift': nrm((L, DEC_BATCH, A_COLS), 1.0),
        'state_pool': nrm((L, DEC_BATCH, POOL_BUF, B_WIDTH), 1.0),
        'state_delta': nrm((L, DEC_BATCH, C_HEADS, C_HEAD, C_HEAD), 0.1),
        'state_delta_conv': nrm((L, DEC_BATCH, CONV_W - 1, 3 * C_WIDTH), 1.0),
        'w_in': nrm((L, D_MODEL, N_IN), D_MODEL ** -0.5),
        'rwkv_mu': unif((L, A_COLS), 0.0, 1.0),
        'rwkv_w0': unif((L, A_WIDTH), -5.0, 1.0),
        'rwkv_w_up': nrm((L, A_W_RANK, A_WIDTH), 0.1),
        'rwkv_a0': nrm((L, A_WIDTH), 0.1),
        'rwkv_a_up': nrm((L, A_A_RANK, A_WIDTH), A_A_RANK ** -0.5),
        'rwkv_g_up': nrm((L, A_G_RANK, A_WIDTH), A_G_RANK ** -0.5),
        'rwkv_k_k': 0.85 + nrm((L, A_WIDTH), 0.02),
        'rwkv_k_a': 1.0 + nrm((L, A_WIDTH), 0.02),
        'rwkv_r_k': nrm((L, A_HEADS, A_HEAD), 0.1),
        'rwkv_gn_w': 1.0 + nrm((L, A_WIDTH), 0.02),
        'rwkv_gn_b': nrm((L, A_WIDTH), 0.02),
        'pool_w': nrm((L, B_GROUPS, B_GROUP, B_GROUP), B_GROUP ** -0.5),
        'pool_scale': 1.0 + nrm((L, B_WIDTH), 0.02),
        'delta_conv_w': nrm((L, CONV_W, 3 * C_WIDTH), 0.5),
        'delta_a_log': jnp.log(unif((L, C_HEADS), 1.0, 16.0)),
        'delta_dt_bias': nrm((L, C_HEADS), 0.1),
        'delta_norm_w': 1.0 + nrm((L, C_HEAD), 0.02),
        'mem_w_kv': nrm((L, D_MODEL, 2 * M_WIDTH), D_MODEL ** -0.5),
        'w_branch': nrm((L, BR_WIDTH, D_MODEL), BETA * A_WIDTH ** -0.5),
        'w_out': nrm((L, D_MODEL, D_MODEL), BETA * D_MODEL ** -0.5),
        'ffn1_w_in': nrm((L, D_MODEL, 2 * D_FF), D_MODEL ** -0.5),
        'ffn1_w_out': nrm((L, D_FF, D_MODEL), BETA * D_FF ** -0.5),
        'ffn2_w_in': nrm((L, D_MODEL, 2 * D_FF), D_MODEL ** -0.5),
        'ffn2_w_out': nrm((L, D_FF, D_MODEL), BETA * D_FF ** -0.5),
        'ln_g': 1.0 + nrm((L, 3, D_MODEL), 0.02),
        'ln_b': nrm((L, 3, D_MODEL), 0.02),
    }


def reference(x_prompt, x_sample, mem_prompt, cache_mem_k, cache_mem_v, state_rwkv, state_rwkv_shift,
              state_pool, state_delta, state_delta_conv, w_in, rwkv_mu, rwkv_w0, rwkv_w_up, rwkv_a0,
              rwkv_a_up, rwkv_g_up, rwkv_k_k, rwkv_k_a, rwkv_r_k, rwkv_gn_w, rwkv_gn_b, pool_w, pool_scale,
              delta_conv_w, delta_a_log, delta_dt_bias, delta_norm_w, mem_w_kv, w_branch, w_out,
              ffn1_w_in, ffn1_w_out, ffn2_w_in, ffn2_w_out, ln_g, ln_b):
    params = [dict(w_in=w_in[l], rwkv_mu=rwkv_mu[l], rwkv_w0=rwkv_w0[l], rwkv_w_up=rwkv_w_up[l],
                   rwkv_a0=rwkv_a0[l], rwkv_a_up=rwkv_a_up[l], rwkv_g_up=rwkv_g_up[l], rwkv_k_k=rwkv_k_k[l],
                   rwkv_k_a=rwkv_k_a[l], rwkv_r_k=rwkv_r_k[l], rwkv_gn_w=rwkv_gn_w[l], rwkv_gn_b=rwkv_gn_b[l],
                   pool_w=pool_w[l], pool_scale=pool_scale[l], delta_conv_w=delta_conv_w[l],
                   delta_a_log=delta_a_log[l], delta_dt_bias=delta_dt_bias[l], delta_norm_w=delta_norm_w[l],
                   w_branch=w_branch[l], w_out=w_out[l], ffn1_w_in=ffn1_w_in[l], ffn1_w_out=ffn1_w_out[l],
                   ffn2_w_in=ffn2_w_in[l], ffn2_w_out=ffn2_w_out[l], ln_g=ln_g[l], ln_b=ln_b[l])
              for l in range(DEPTH)]
    bp = x_prompt.shape[0]
    n_mem = mem_prompt.shape[1]
    dt = x_prompt.dtype
    mem_kv = jnp.einsum('bmd,lde->lbme', mem_prompt, mem_w_kv)
    p_mem_k = mem_kv[..., :M_WIDTH].reshape(DEPTH, bp, n_mem, M_HEADS, M_HEAD)
    p_mem_v = mem_kv[..., M_WIDTH:].reshape(DEPTH, bp, n_mem, M_HEADS, M_HEAD)
    y_prompt, p_rwkv_shift, p_rwkv, p_pool, p_delta_conv, p_delta = run_trunk(
        x_prompt, p_mem_k, p_mem_v,
        jnp.zeros((DEPTH, bp, A_COLS), dt),
        jnp.zeros((DEPTH, bp, A_HEADS, A_HEAD, A_HEAD), dt),
        jnp.zeros((DEPTH, bp, POOL_BUF, B_WIDTH), dt),
        jnp.zeros((DEPTH, bp, CONV_W - 1, 3 * C_WIDTH), dt),
        jnp.zeros((DEPTH, bp, C_HEADS, C_HEAD, C_HEAD), dt),
        params, 0)
    y_sample, s_rwkv_shift, s_rwkv, s_pool, s_delta_conv, s_delta = run_trunk(
        x_sample, cache_mem_k, cache_mem_v, state_rwkv_shift, state_rwkv, state_pool, state_delta_conv,
        state_delta, params, PAST_LEN)
    return (y_prompt, y_sample, p_rwkv, p_rwkv_shift, p_pool, p_delta, p_delta_conv, p_mem_k, p_mem_v,
            s_rwkv, s_rwkv_shift, s_pool, s_delta, s_delta_conv)
```

```python
import functools
import math

import jax
import jax.numpy as jnp
from jax import lax
from jax.experimental import pallas as pl
from jax.experimental.pallas import tpu as pltpu

F32 = jnp.float32
BF16 = jnp.bfloat16

D_MODEL = 1024
DEPTH = 4
PAST_LEN = 16384
A_WIDTH = 512
A_HEAD = 64
A_HEADS = 8
A_LORA = 128
A_G_RANK = 128
A_COLS = 3 * A_WIDTH + A_LORA + A_G_RANK
GN_EPS = 64e-5
B_WIDTH = 512
B_GROUP = 128
POOL_WINDOWS = (2, 4, 8, 16)
POOL_BUF = 15
C_WIDTH = 512
C_HEAD = 128
C_HEADS = 4
CONV_W = 4
C_COLS = 4 * C_WIDTH + 2 * C_HEADS
C_COLS_PAD = 4 * C_WIDTH + 128
N_MEM = 256
M_HEADS = 4
M_HEAD = 64
M_WIDTH = 256
N_BRANCH = 4
D_FF = 2048
FF_CHUNK = 512
ALPHA = (2.0 * DEPTH) ** 0.25
LN_EPS = 1e-5
EXP_M05 = math.exp(-0.5)

OFF_A = 0
OFF_B = OFF_A + A_COLS
OFF_C = OFF_B + B_WIDTH
OFF_M = OFF_C + C_COLS
OFF_G = OFF_M + M_WIDTH

VMEM_LIMIT = 56 * 1024 * 1024


def _dot(a, b):
    return jnp.dot(a.astype(BF16), b.astype(BF16), preferred_element_type=F32)


def _dot_nt(a, b):
    return lax.dot_general(a.astype(BF16), b.astype(BF16), (((1,), (1,)), ((), ())),
                           preferred_element_type=F32)


def _dot_tn(a, b):
    return lax.dot_general(a.astype(BF16), b.astype(BF16), (((0,), (0,)), ((), ())),
                           preferred_element_type=F32)


def _bf16_pieces(x, n):
    pieces = []
    rest = x
    for _ in range(n):
        p = rest.astype(BF16)
        pieces.append(p)
        rest = rest - p.astype(F32)
    return pieces


def _dot_exact_lhs(m, x):
    out = None
    for p in _bf16_pieces(x, 3):
        d = jnp.dot(m, p, preferred_element_type=F32)
        out = d if out is None else out + d
    return out


def _dot_exact_rhs(x, m):
    out = None
    for p in _bf16_pieces(x, 2):
        d = jnp.dot(p, m, preferred_element_type=F32)
        out = d if out is None else out + d
    return out


def _layer_norm(z, g, b):
    mu = jnp.mean(z, axis=-1, keepdims=True)
    zc = z - mu
    var = jnp.mean(zc * zc, axis=-1, keepdims=True)
    return zc * lax.rsqrt(var + LN_EPS) * g + b


def _sigmoid(x):
    return 1.0 / (1.0 + jnp.exp(-x))


def _silu(x):
    return x * _sigmoid(x)


def _softplus(x):
    return jnp.maximum(x, 0.0) + jnp.log(1.0 + jnp.exp(-jnp.abs(x)))


def _tri_masks(c):
    row = lax.broadcasted_iota(jnp.int32, (c, c), 0)
    col = lax.broadcasted_iota(jnp.int32, (c, c), 1)
    return row > col, row >= col, (row == col).astype(F32)


def _segment_cumsum_mask(rows, c):
    shift = c.bit_length() - 1
    row = lax.broadcasted_iota(jnp.int32, (rows, rows), 0)
    col = lax.broadcasted_iota(jnp.int32, (rows, rows), 1)
    same = (row >> shift) == (col >> shift)
    return jnp.where(same & (col <= row), 1.0, 0.0).astype(BF16)


def _tri_inv(a, eye, c):
    t = eye + a
    p = a
    k = 2
    while k < c:
        p = _dot(p, p)
        t = t + _dot(t, p)
        k *= 2
    return t


def _ffn_kernel(x_ref, win_ref, wout_ref, g_ref, b_ref, o_ref):
    x = x_ref[...]
    xb = x.astype(BF16)
    acc = None
    for j in range(D_FF // FF_CHUNK):
        lo = j * FF_CHUNK
        gate = jnp.dot(xb, win_ref[:, lo:lo + FF_CHUNK], preferred_element_type=F32)
        up = jnp.dot(xb, win_ref[:, D_FF + lo:D_FF + lo + FF_CHUNK], preferred_element_type=F32)
        part = _dot(_silu(gate) * up, wout_ref[lo:lo + FF_CHUNK, :])
        acc = part if acc is None else acc + part
    o_ref[...] = _layer_norm(ALPHA * x + 0.5 * acc, g_ref[...], b_ref[...])


def _ffn(x, w_in, w_out, g, b, layer, idx, tm):
    rows = x.shape[0]
    const = lambda i: (layer, 0, 0)
    return pl.pallas_call(
        _ffn_kernel,
        grid=(rows // tm,),
        in_specs=[
            pl.BlockSpec((tm, D_MODEL), lambda i: (i, 0)),
            pl.BlockSpec((None, D_MODEL, 2 * D_FF), const),
            pl.BlockSpec((None, D_FF, D_MODEL), const),
            pl.BlockSpec((None, 1, D_MODEL), lambda i: (layer * 3 + idx, 0, 0)),
            pl.BlockSpec((None, 1, D_MODEL), lambda i: (layer * 3 + idx, 0, 0)),
        ],
        out_specs=pl.BlockSpec((tm, D_MODEL), lambda i: (i, 0)),
        out_shape=jax.ShapeDtypeStruct((rows, D_MODEL), F32),
        compiler_params=pltpu.CompilerParams(
            dimension_semantics=("parallel",), vmem_limit_bytes=VMEM_LIMIT),
        name="ffn",
    )(x, w_in, w_out, g, b)


def _merge_kernel(x_ref, oa_ref, ob_ref, oc_ref, om_ref, wg_ref, wba_ref, wbb_ref, wbc_ref, wbm_ref,
                  wo_ref, g_ref, b_ref, o_ref):
    x = x_ref[...]
    xb = x.astype(BF16)
    merged = None
    for i, (br_ref, wb_ref) in enumerate(
            ((oa_ref, wba_ref), (ob_ref, wbb_ref), (oc_ref, wbc_ref), (om_ref, wbm_ref))):
        gate = _sigmoid(jnp.dot(xb, wg_ref[:, i * D_MODEL:(i + 1) * D_MODEL], preferred_element_type=F32))
        term = gate * jnp.dot(br_ref[...], wb_ref[...], preferred_element_type=F32)
        merged = term if merged is None else merged + term
    mix = _dot(merged, wo_ref[...])
    o_ref[...] = _layer_norm(ALPHA * x + mix, g_ref[...], b_ref[...])


def _merge(x, oa, ob, oc, om, wts, layer, tm):
    rows = x.shape[0]
    const = lambda i: (layer, 0, 0)
    row_spec = lambda w: pl.BlockSpec((tm, w), lambda i: (i, 0))
    return pl.pallas_call(
        _merge_kernel,
        grid=(rows // tm,),
        in_specs=[
            row_spec(D_MODEL), row_spec(A_WIDTH), row_spec(B_WIDTH), row_spec(C_WIDTH), row_spec(M_WIDTH),
            pl.BlockSpec((None, D_MODEL, N_BRANCH * D_MODEL), const),
            pl.BlockSpec((None, A_WIDTH, D_MODEL), const),
            pl.BlockSpec((None, B_WIDTH, D_MODEL), const),
            pl.BlockSpec((None, C_WIDTH, D_MODEL), const),
            pl.BlockSpec((None, M_WIDTH, D_MODEL), const),
            pl.BlockSpec((None, D_MODEL, D_MODEL), const),
            pl.BlockSpec((None, 1, D_MODEL), lambda i: (layer * 3 + 1, 0, 0)),
            pl.BlockSpec((None, 1, D_MODEL), lambda i: (layer * 3 + 1, 0, 0)),
        ],
        out_specs=row_spec(D_MODEL),
        out_shape=jax.ShapeDtypeStruct((rows, D_MODEL), F32),
        compiler_params=pltpu.CompilerParams(
            dimension_semantics=("parallel",), vmem_limit_bytes=VMEM_LIMIT),
        name="merge",
    )(x, oa, ob, oc, om, wts["w_g"], wts["wb_a"], wts["wb_b"], wts["wb_c"], wts["wb_m"], wts["w_out"],
      wts["ln_g"], wts["ln_b"])


def _memkv_kernel(m_ref, w_ref, k_ref, v_ref):
    kv = _dot(m_ref[...], w_ref[...])
    k_ref[...] = kv[:, :M_WIDTH]
    v_ref[...] = kv[:, M_WIDTH:]


def _memkv(mem, w_kv):
    rows = mem.shape[0]
    out = jax.ShapeDtypeStruct((DEPTH, rows, M_WIDTH), F32)
    return pl.pallas_call(
        _memkv_kernel,
        grid=(DEPTH,),
        in_specs=[
            pl.BlockSpec((rows, D_MODEL), lambda l: (0, 0)),
            pl.BlockSpec((None, D_MODEL, 2 * M_WIDTH), lambda l: (l, 0, 0)),
        ],
        out_specs=[pl.BlockSpec((None, rows, M_WIDTH), lambda l: (l, 0, 0))] * 2,
        out_shape=[out, out],
        compiler_params=pltpu.CompilerParams(
            dimension_semantics=("parallel",), vmem_limit_bytes=VMEM_LIMIT),
        name="memkv",
    )(mem, w_kv)


def _rwkv_kernel(x_ref, shift_ref, s0_ref, wa_ref, mu_ref, w0_ref, wup_ref, a0_ref, aup_ref, gup_ref,
                 kk_ref, ka_ref, rk_ref, gnw_ref, gnb_ref, e_ref,
                 o_ref, shift_out_ref, s_out_ref,
                 hist_ref, rt_ref, kt_ref, at_ref, bt_ref, v_ref, ec_ref, y_ref, *, nb, lt, c):
    t = pl.program_id(1)
    rows = nb * lt
    cps = lt // c

    @pl.when(t == 0)
    def _():
        hist_ref[:, 7:8, :] = shift_ref[...]
        s_out_ref[...] = s0_ref[...]

    pa = _dot(x_ref[...], wa_ref[...])
    hist_ref[:, 8:8 + lt, :] = pa.reshape(nb, lt, A_COLS)
    prev = hist_ref[:, 7:7 + lt, :].reshape(rows, A_COLS)
    last = hist_ref[:, 7 + lt:8 + lt, :]
    shift_out_ref[...] = last
    hist_ref[:, 7:8, :] = last

    xm = pa + (prev - pa) * mu_ref[...]
    r = xm[:, 0:A_WIDTH]
    k = xm[:, A_WIDTH:2 * A_WIDTH]
    v = xm[:, 2 * A_WIDTH:3 * A_WIDTH]
    xwa = xm[:, 3 * A_WIDTH:3 * A_WIDTH + A_LORA]
    xg = xm[:, 3 * A_WIDTH + A_LORA:]
    z = w0_ref[...] + _dot(jnp.tanh(xwa), wup_ref[...])
    lw = -EXP_M05 * _sigmoid(z)
    a = _sigmoid(a0_ref[...] + _dot(xwa, aup_ref[...]))
    g = _dot(_sigmoid(xg), gup_ref[...])
    e = e_ref[...]
    kkr = k * kk_ref[...]
    kk = kkr * lax.rsqrt(_dot_exact_rhs(kkr * kkr, e) + 1e-12)
    k2 = k * (1.0 + (a - 1.0) * ka_ref[...])
    cum = _dot_exact_lhs(_segment_cumsum_mask(rows, c), lw)
    ecum = jnp.exp(cum)
    encum = jnp.exp(-cum)
    rt_ref[...] = r * ecum
    kt_ref[...] = k2 * encum
    bt_ref[...] = kk * a * encum
    at_ref[...] = -kk * jnp.exp(cum - lw)
    v_ref[...] = v
    ec_ref[...] = ecum

    strict, incl, eye = _tri_masks(c)

    def chunk(i, carry):
        row0 = pl.multiple_of(i * c, c)
        rs = pl.ds(row0, c)
        b = i // cps
        rt_c, kt_c, at_c, bt_c, v_c = rt_ref[rs, :], kt_ref[rs, :], at_ref[rs, :], bt_ref[rs, :], v_ref[rs, :]
        w_end = ec_ref[pl.ds(row0 + c - 1, 1), :]
        for h in range(A_HEADS):
            sl = slice(h * A_HEAD, (h + 1) * A_HEAD)
            rt, kt, at, bt, vh = rt_c[:, sl], kt_c[:, sl], at_c[:, sl], bt_c[:, sl], v_c[:, sl]
            ar = jnp.concatenate([at, rt], axis=0)
            gb = _dot_nt(ar, bt)
            gk = _dot_nt(ar, kt)
            a_ab = jnp.where(strict, gb[:c], 0.0)
            a_rb = jnp.where(incl, gb[c:], 0.0)
            a_ak = jnp.where(strict, gk[:c], 0.0)
            a_rk = jnp.where(incl, gk[c:], 0.0)
            tinv = _tri_inv(a_ab, eye, c)
            ahat = _dot(tinv, at)
            uv = _dot(tinv, _dot(a_ak, vh))
            s = s_out_ref[b, h]
            ps = _dot_nt(jnp.concatenate([ahat, rt], axis=0), s)
            u = ps[:c] + uv
            y = ps[c:] + _dot(a_rb, u) + _dot(a_rk, vh)
            uvh = jnp.concatenate([u, vh], axis=0)
            bk = jnp.concatenate([bt, kt], axis=0)
            s_out_ref[b, h] = (s + _dot_tn(uvh, bk)) * w_end[:, sl]
            y_ref[rs, sl] = y
        return carry

    lax.fori_loop(0, rows // c, chunk, 0)

    y = y_ref[...]
    inv_n = 1.0 / A_HEAD
    mean = _dot_exact_rhs(y, e) * inv_n
    yc = y - mean
    var = _dot_exact_rhs(yc * yc, e) * inv_n
    yn = yc * lax.rsqrt(var + GN_EPS) * gnw_ref[...] + gnb_ref[...]
    bonus = _dot_exact_rhs(r * k2 * rk_ref[...], e) * v
    o_ref[...] = ((yn + bonus) * g).astype(o_ref.dtype)


def _rwkv(x, shift, s0, wts, layer, state_layer, bsz, seq, nb, lt, c):
    rows = nb * lt
    tpb = seq // lt
    wl = lambda i, t: (layer, 0, 0)
    vec = lambda width: pl.BlockSpec((None, 1, width), wl)
    kern = functools.partial(_rwkv_kernel, nb=nb, lt=lt, c=c)
    return pl.pallas_call(
        kern,
        grid=(bsz // nb, tpb),
        in_specs=[
            pl.BlockSpec((rows, D_MODEL), lambda i, t: (i * tpb + t, 0)),
            pl.BlockSpec((None, nb, 1, A_COLS), lambda i, t: (state_layer, i, 0, 0)),
            pl.BlockSpec((None, nb, A_HEADS, A_HEAD, A_HEAD), lambda i, t: (state_layer, i, 0, 0, 0)),
            pl.BlockSpec((None, D_MODEL, A_COLS), wl),
            vec(A_COLS), vec(A_WIDTH),
            pl.BlockSpec((None, A_LORA, A_WIDTH), wl),
            vec(A_WIDTH),
            pl.BlockSpec((None, A_LORA, A_WIDTH), wl),
            pl.BlockSpec((None, A_G_RANK, A_WIDTH), wl),
            vec(A_WIDTH), vec(A_WIDTH), vec(A_WIDTH), vec(A_WIDTH), vec(A_WIDTH),
            pl.BlockSpec((A_WIDTH, A_WIDTH), lambda i, t: (0, 0)),
        ],
        out_specs=[
            pl.BlockSpec((rows, A_WIDTH), lambda i, t: (i * tpb + t, 0)),
            pl.BlockSpec((nb, 1, A_COLS), lambda i, t: (i, 0, 0)),
            pl.BlockSpec((nb, A_HEADS, A_HEAD, A_HEAD), lambda i, t: (i, 0, 0, 0)),
        ],
        out_shape=[
            jax.ShapeDtypeStruct((bsz * seq, A_WIDTH), BF16),
            jax.ShapeDtypeStruct((bsz, 1, A_COLS), F32),
            jax.ShapeDtypeStruct((bsz, A_HEADS, A_HEAD, A_HEAD), F32),
        ],
        scratch_shapes=[pltpu.VMEM((nb, 8 + lt, A_COLS), F32)]
        + [pltpu.VMEM((rows, A_WIDTH), F32)] * 7,
        compiler_params=pltpu.CompilerParams(
            dimension_semantics=("parallel", "arbitrary"), vmem_limit_bytes=VMEM_LIMIT),
        name="rwkv",
    )(x, shift, s0, wts["w_a"], wts["rwkv_mu"], wts["rwkv_w0"], wts["rwkv_w_up"], wts["rwkv_a0"],
      wts["rwkv_a_up"], wts["rwkv_g_up"], wts["rwkv_k_k"], wts["rwkv_k_a"], wts["rwkv_r_k"],
      wts["rwkv_gn_w"], wts["rwkv_gn_b"], wts["head_ones"])


def _pool_kernel(x_ref, buf_ref, wb_ref, pw_ref, ps_ref, o_ref, buf_out_ref, hist_ref, *, nb, lt, pos0):
    t = pl.program_id(1)
    rows = nb * lt

    @pl.when(t == 0)
    def _():
        hist_ref[:, 1:16, :] = buf_ref[...]

    pb = _dot(x_ref[...], wb_ref[...])
    hist_ref[:, 16:16 + lt, :] = pb.reshape(nb, lt, B_WIDTH)
    pos = pos0 + t * lt + lax.broadcasted_iota(jnp.int32, (1, lt, 1), 1)
    parts = []
    for j, w in enumerate(POOL_WINDOWS):
        cs = slice(j * B_GROUP, (j + 1) * B_GROUP)
        cur = hist_ref[:, 16:16 + lt, cs]
        acc = cur
        for d in range(1, w):
            acc = acc + hist_ref[:, 16 - d:16 - d + lt, cs]
        count = jnp.minimum(pos + 1, w).astype(F32)
        parts.append(acc / count - cur)
    pooled = jnp.concatenate(parts, axis=-1).reshape(rows, B_WIDTH)
    o_ref[...] = (_dot(pooled, pw_ref[...]) * ps_ref[...]).astype(o_ref.dtype)
    tail = hist_ref[:, lt:lt + 16, :]
    buf_out_ref[...] = tail[:, 1:16, :]
    hist_ref[:, 0:16, :] = tail


def _pool(x, buf, wts, layer, state_layer, bsz, seq, nb, lt, pos0):
    rows = nb * lt
    tpb = seq // lt
    wl = lambda i, t: (layer, 0, 0)
    kern = functools.partial(_pool_kernel, nb=nb, lt=lt, pos0=pos0)
    return pl.pallas_call(
        kern,
        grid=(bsz // nb, tpb),
        in_specs=[
            pl.BlockSpec((rows, D_MODEL), lambda i, t: (i * tpb + t, 0)),
            pl.BlockSpec((None, nb, POOL_BUF, B_WIDTH), lambda i, t: (state_layer, i, 0, 0)),
            pl.BlockSpec((None, D_MODEL, B_WIDTH), wl),
            pl.BlockSpec((None, B_WIDTH, B_WIDTH), wl),
            pl.BlockSpec((None, 1, B_WIDTH), wl),
        ],
        out_specs=[
            pl.BlockSpec((rows, B_WIDTH), lambda i, t: (i * tpb + t, 0)),
            pl.BlockSpec((nb, POOL_BUF, B_WIDTH), lambda i, t: (i, 0, 0)),
        ],
        out_shape=[
            jax.ShapeDtypeStruct((bsz * seq, B_WIDTH), BF16),
            jax.ShapeDtypeStruct((bsz, POOL_BUF, B_WIDTH), F32),
        ],
        scratch_shapes=[pltpu.VMEM((nb, 16 + lt, B_WIDTH), F32)],
        compiler_params=pltpu.CompilerParams(
            dimension_semantics=("parallel", "arbitrary"), vmem_limit_bytes=VMEM_LIMIT),
        name="pool",
    )(x, buf, wts["w_b"], wts["pool_w"], wts["pool_scale"])


def _delta_kernel(x_ref, conv_ref, s0_ref, wc_ref, cw_ref, alog_ref, dtb_ref, nw_ref,
                  o_ref, conv_out_ref, s_out_ref,
                  hist_ref, q_ref, k_ref, v_ref, z_ref, beta_ref, cum_ref, y_ref, *, nb, lt, c):
    t = pl.program_id(1)
    rows = nb * lt
    cps = lt // c
    qkv_w = 3 * C_WIDTH

    @pl.when(t == 0)
    def _():
        hist_ref[:, 5:8, :] = conv_ref[...]
        s_out_ref[...] = s0_ref[...]

    pc = _dot(x_ref[...], wc_ref[...])
    hist_ref[:, 8:8 + lt, :] = pc[:, :qkv_w].reshape(nb, lt, qkv_w)
    z_ref[...] = pc[:, qkv_w:qkv_w + C_WIDTH]
    ba = pc[:, qkv_w + C_WIDTH:]
    conv = None
    for j in range(CONV_W):
        term = hist_ref[:, 5 + j:5 + j + lt, :] * cw_ref[j:j + 1, :]
        conv = term if conv is None else conv + term
    tail = hist_ref[:, 5 + lt:8 + lt, :]
    conv_out_ref[...] = tail
    hist_ref[:, 5:8, :] = tail
    act = _silu(conv).reshape(rows, qkv_w)
    for h in range(C_HEADS):
        qs = slice(h * C_HEAD, (h + 1) * C_HEAD)
        ks = slice(C_WIDTH + h * C_HEAD, C_WIDTH + (h + 1) * C_HEAD)
        qh = act[:, qs]
        kh = act[:, ks]
        q_ref[:, qs] = qh * (lax.rsqrt(jnp.sum(qh * qh, axis=-1, keepdims=True) + 1e-12) * C_HEAD ** -0.5)
        k_ref[:, qs] = kh * lax.rsqrt(jnp.sum(kh * kh, axis=-1, keepdims=True) + 1e-12)
    v_ref[...] = act[:, 2 * C_WIDTH:]
    beta_ref[...] = _sigmoid(ba)
    g = -jnp.exp(alog_ref[...]) * _softplus(ba + dtb_ref[...])
    cum_ref[...] = _dot_exact_lhs(_segment_cumsum_mask(rows, c), g)

    strict, incl, eye = _tri_masks(c)

    def chunk(i, carry):
        row0 = pl.multiple_of(i * c, c)
        rs = pl.ds(row0, c)
        b = i // cps
        q_c, k_c, v_c = q_ref[rs, :], k_ref[rs, :], v_ref[rs, :]
        beta_c = beta_ref[rs, :]
        cum_c = cum_ref[rs, :]
        cum_t = cum_c.T
        cum_end = cum_ref[pl.ds(row0 + c - 1, 1), :]
        for h in range(C_HEADS):
            sl = slice(h * C_HEAD, (h + 1) * C_HEAD)
            qh, kh, vh = q_c[:, sl], k_c[:, sl], v_c[:, sl]
            bh = beta_c[:, h:h + 1]
            col = cum_c[:, C_HEADS + h:C_HEADS + h + 1]
            rowv = cum_t[C_HEADS + h:C_HEADS + h + 1, :]
            g_end = cum_end[:, C_HEADS + h:C_HEADS + h + 1]
            decay = jnp.where(incl, jnp.exp(jnp.where(incl, col - rowv, 0.0)), 0.0)
            kb = kh * bh
            gm = _dot_nt(jnp.concatenate([kb, qh], axis=0), kh)
            lmat = jnp.where(strict, gm[:c] * decay, 0.0)
            aqk = gm[c:] * decay
            tinv = _tri_inv(-lmat, eye, c)
            ecol = jnp.exp(col)
            u = _dot(tinv, vh * bh)
            w = _dot(tinv, kb * ecol)
            s = s_out_ref[b, h]
            ps = _dot(jnp.concatenate([w, qh * ecol], axis=0), s)
            v_new = u - ps[:c]
            y_ref[rs, sl] = ps[c:] + _dot(aqk, v_new)
            s_out_ref[b, h] = s * jnp.exp(g_end) + _dot_tn(kh * jnp.exp(g_end - col), v_new)
        return carry

    lax.fori_loop(0, rows // c, chunk, 0)

    for h in range(C_HEADS):
        sl = slice(h * C_HEAD, (h + 1) * C_HEAD)
        oh = y_ref[:, sl]
        oh = oh * lax.rsqrt(jnp.mean(oh * oh, axis=-1, keepdims=True) + 1e-6) * nw_ref[...]
        o_ref[:, sl] = (oh * _silu(z_ref[:, sl])).astype(o_ref.dtype)


def _delta(x, conv, s0, wts, layer, state_layer, bsz, seq, nb, lt, c):
    rows = nb * lt
    tpb = seq // lt
    wl = lambda i, t: (layer, 0, 0)
    qkv_w = 3 * C_WIDTH
    kern = functools.partial(_delta_kernel, nb=nb, lt=lt, c=c)
    return pl.pallas_call(
        kern,
        grid=(bsz // nb, tpb),
        in_specs=[
            pl.BlockSpec((rows, D_MODEL), lambda i, t: (i * tpb + t, 0)),
            pl.BlockSpec((None, nb, CONV_W - 1, qkv_w), lambda i, t: (state_layer, i, 0, 0)),
            pl.BlockSpec((None, nb, C_HEADS, C_HEAD, C_HEAD), lambda i, t: (state_layer, i, 0, 0, 0)),
            pl.BlockSpec((None, D_MODEL, C_COLS_PAD), wl),
            pl.BlockSpec((None, CONV_W, qkv_w), wl),
            pl.BlockSpec((None, 1, 128), wl),
            pl.BlockSpec((None, 1, 128), wl),
            pl.BlockSpec((None, 1, C_HEAD), wl),
        ],
        out_specs=[
            pl.BlockSpec((rows, C_WIDTH), lambda i, t: (i * tpb + t, 0)),
            pl.BlockSpec((nb, CONV_W - 1, qkv_w), lambda i, t: (i, 0, 0)),
            pl.BlockSpec((nb, C_HEADS, C_HEAD, C_HEAD), lambda i, t: (i, 0, 0, 0)),
        ],
        out_shape=[
            jax.ShapeDtypeStruct((bsz * seq, C_WIDTH), BF16),
            jax.ShapeDtypeStruct((bsz, CONV_W - 1, qkv_w), F32),
            jax.ShapeDtypeStruct((bsz, C_HEADS, C_HEAD, C_HEAD), F32),
        ],
        scratch_shapes=[pltpu.VMEM((nb, 8 + lt, qkv_w), F32)]
        + [pltpu.VMEM((rows, C_WIDTH), F32)] * 4
        + [pltpu.VMEM((rows, 128), F32)] * 2
        + [pltpu.VMEM((rows, C_WIDTH), F32)],
        compiler_params=pltpu.CompilerParams(
            dimension_semantics=("parallel", "arbitrary"), vmem_limit_bytes=VMEM_LIMIT),
        name="delta",
    )(x, conv, s0, wts["w_c"], wts["delta_conv_w"], wts["delta_a_log"], wts["delta_dt_bias"],
      wts["delta_norm_w"])


def _mem_kernel(x_ref, k_ref, v_ref, wm_ref, o_ref, q_ref, *, nb, lt):
    q_ref[...] = _dot(x_ref[...], wm_ref[...]) * (M_HEAD ** -0.5)
    lane = lax.broadcasted_iota(jnp.int32, (1, M_WIDTH), 1)

    def seq_body(b, carry):
        rs = pl.ds(pl.multiple_of(b * lt, lt), lt)
        q = q_ref[rs, :]
        keys = k_ref[b].astype(BF16)
        vals = v_ref[b].astype(BF16)
        out = jnp.zeros((lt, M_WIDTH), F32)
        for h in range(M_HEADS):
            head = (lane >= h * M_HEAD) & (lane < (h + 1) * M_HEAD)
            s = _dot_nt(jnp.where(head, q, 0.0), keys)
            p = jnp.exp(s - jnp.max(s, axis=-1, keepdims=True))
            p = p / jnp.sum(p, axis=-1, keepdims=True)
            out = jnp.where(head, _dot(p, vals), out)
        o_ref[rs, :] = out.astype(o_ref.dtype)
        return carry

    lax.fori_loop(0, nb, seq_body, 0)


def _mem(x, mem_k, mem_v, wts, layer, bsz, seq, nb, lt):
    rows = nb * lt
    tpb = seq // lt
    kern = functools.partial(_mem_kernel, nb=nb, lt=lt)
    kv_spec = pl.BlockSpec((None, nb, N_MEM, M_WIDTH), lambda i, t: (layer, i, 0, 0))
    return pl.pallas_call(
        kern,
        grid=(bsz // nb, tpb),
        in_specs=[
            pl.BlockSpec((rows, D_MODEL), lambda i, t: (i * tpb + t, 0)),
            kv_spec, kv_spec,
            pl.BlockSpec((None, D_MODEL, M_WIDTH), lambda i, t: (layer, 0, 0)),
        ],
        out_specs=pl.BlockSpec((rows, M_WIDTH), lambda i, t: (i * tpb + t, 0)),
        out_shape=jax.ShapeDtypeStruct((bsz * seq, M_WIDTH), BF16),
        scratch_shapes=[pltpu.VMEM((rows, M_WIDTH), F32)],
        compiler_params=pltpu.CompilerParams(
            dimension_semantics=("parallel", "arbitrary"), vmem_limit_bytes=VMEM_LIMIT),
        name="mem",
    )(x, mem_k, mem_v, wts["w_m"])


def _prepare_weights(w_in, rwkv_mu, rwkv_w0, rwkv_w_up, rwkv_a0, rwkv_a_up, rwkv_g_up, rwkv_k_k, rwkv_k_a,
                     rwkv_r_k, rwkv_gn_w, rwkv_gn_b, pool_w, pool_scale, delta_conv_w, delta_a_log,
                     delta_dt_bias, delta_norm_w, w_branch, w_out, ffn1_w_in, ffn1_w_out, ffn2_w_in,
                     ffn2_w_out, ln_g, ln_b):
    depth = w_in.shape[0]
    row = lambda p: p.reshape(depth, 1, -1).astype(F32)
    zeros_lora = jnp.zeros((depth, A_LORA // 2, A_WIDTH), F32)
    lane_row = lambda p: jnp.pad(p.astype(F32), ((0, 0), (C_HEADS, 128 - 2 * C_HEADS))).reshape(depth, 1, 128)
    pool_bd = jnp.zeros((depth, B_WIDTH, B_WIDTH), F32)
    for gidx in range(len(POOL_WINDOWS)):
        sl = slice(gidx * B_GROUP, (gidx + 1) * B_GROUP)
        pool_bd = pool_bd.at[:, sl, sl].set(pool_w[:, gidx].astype(F32))
    head = jnp.arange(A_WIDTH) // A_HEAD
    w_c = jnp.pad(w_in[:, :, OFF_C:OFF_M], ((0, 0), (0, 0), (0, C_COLS_PAD - C_COLS)))
    return {
        "w_a": w_in[:, :, OFF_A:OFF_B].astype(BF16),
        "w_b": w_in[:, :, OFF_B:OFF_C].astype(BF16),
        "w_c": w_c.astype(BF16),
        "w_m": w_in[:, :, OFF_M:OFF_G].astype(BF16),
        "w_g": w_in[:, :, OFF_G:].astype(BF16),
        "rwkv_mu": row(rwkv_mu), "rwkv_w0": row(rwkv_w0), "rwkv_a0": row(rwkv_a0),
        "rwkv_w_up": jnp.concatenate([rwkv_w_up.astype(F32), zeros_lora], axis=1).astype(BF16),
        "rwkv_a_up": jnp.concatenate([zeros_lora, rwkv_a_up.astype(F32)], axis=1).astype(BF16),
        "rwkv_g_up": rwkv_g_up.astype(BF16),
        "rwkv_k_k": row(rwkv_k_k), "rwkv_k_a": row(rwkv_k_a), "rwkv_r_k": row(rwkv_r_k),
        "rwkv_gn_w": row(rwkv_gn_w), "rwkv_gn_b": row(rwkv_gn_b),
        "head_ones": (head[:, None] == head[None, :]).astype(BF16),
        "pool_w": pool_bd.astype(BF16), "pool_scale": row(pool_scale),
        "delta_conv_w": delta_conv_w.astype(F32),
        "delta_a_log": lane_row(delta_a_log), "delta_dt_bias": lane_row(delta_dt_bias),
        "delta_norm_w": row(delta_norm_w),
        "wb_a": w_branch[:, 0:A_WIDTH].astype(BF16),
        "wb_b": w_branch[:, A_WIDTH:A_WIDTH + B_WIDTH].astype(BF16),
        "wb_c": w_branch[:, A_WIDTH + B_WIDTH:A_WIDTH + B_WIDTH + C_WIDTH].astype(BF16),
        "wb_m": w_branch[:, A_WIDTH + B_WIDTH + C_WIDTH:].astype(BF16),
        "w_out": w_out.astype(BF16),
        "ffn1_w_in": ffn1_w_in.astype(BF16), "ffn1_w_out": ffn1_w_out.astype(BF16),
        "ffn2_w_in": ffn2_w_in.astype(BF16), "ffn2_w_out": ffn2_w_out.astype(BF16),
        "ln_g": ln_g.reshape(depth * 3, 1, D_MODEL).astype(F32),
        "ln_b": ln_b.reshape(depth * 3, 1, D_MODEL).astype(F32),
    }


def _trunk(x, mem_k, mem_v, shift, s_rwkv, pool_buf, conv_buf, s_delta, wts, bsz, seq, pos0, stacked,
           tm, nb, lt, c, nb_mem, lt_mem):
    new = []
    for l in range(DEPTH):
        sl = l if stacked else 0
        x = _ffn(x, wts["ffn1_w_in"], wts["ffn1_w_out"], wts["ln_g"], wts["ln_b"], l, 0, tm)
        oa, shift_n, rwkv_n = _rwkv(x, shift, s_rwkv, wts, l, sl, bsz, seq, nb, lt, c)
        ob, pool_n = _pool(x, pool_buf, wts, l, sl, bsz, seq, nb, lt, pos0)
        oc, conv_n, delta_n = _delta(x, conv_buf, s_delta, wts, l, sl, bsz, seq, nb, lt, c)
        om = _mem(x, mem_k, mem_v, wts, l, bsz, seq, nb_mem, lt_mem)
        x = _merge(x, oa, ob, oc, om, wts, l, tm)
        x = _ffn(x, wts["ffn2_w_in"], wts["ffn2_w_out"], wts["ln_g"], wts["ln_b"], l, 2, tm)
        new.append((rwkv_n, shift_n.reshape(bsz, A_COLS), pool_n, delta_n, conv_n))
    return (x,) + tuple(jnp.stack([st[i] for st in new]) for i in range(5))


def kernel(x_prompt, x_sample, mem_prompt, cache_mem_k, cache_mem_v, state_rwkv, state_rwkv_shift, state_pool, state_delta, state_delta_conv, w_in, rwkv_mu, rwkv_w0, rwkv_w_up, rwkv_a0, rwkv_a_up, rwkv_g_up, rwkv_k_k, rwkv_k_a, rwkv_r_k, rwkv_gn_w, rwkv_gn_b, pool_w, pool_scale, delta_conv_w, delta_a_log, delta_dt_bias, delta_norm_w, mem_w_kv, w_branch, w_out, ffn1_w_in, ffn1_w_out, ffn2_w_in, ffn2_w_out, ln_g, ln_b):
    wts = _prepare_weights(w_in, rwkv_mu, rwkv_w0, rwkv_w_up, rwkv_a0, rwkv_a_up, rwkv_g_up, rwkv_k_k,
                           rwkv_k_a, rwkv_r_k, rwkv_gn_w, rwkv_gn_b, pool_w, pool_scale, delta_conv_w,
                           delta_a_log, delta_dt_bias, delta_norm_w, w_branch, w_out, ffn1_w_in,
                           ffn1_w_out, ffn2_w_in, ffn2_w_out, ln_g, ln_b)
    bp, sp, _ = x_prompt.shape
    bs, ss, _ = x_sample.shape
    n_mem = mem_prompt.shape[1]

    mem_k, mem_v = _memkv(mem_prompt.reshape(bp * n_mem, D_MODEL), mem_w_kv.astype(BF16))
    mem_k = mem_k.reshape(DEPTH, bp, n_mem, M_WIDTH)
    mem_v = mem_v.reshape(DEPTH, bp, n_mem, M_WIDTH)

    zeros = lambda *shape: jnp.zeros((1, bp) + shape, F32)
    lt_p = min(sp, 256)
    c_p = min(lt_p, 64)
    prompt = _trunk(
        x_prompt.reshape(bp * sp, D_MODEL), mem_k, mem_v,
        zeros(1, A_COLS), zeros(A_HEADS, A_HEAD, A_HEAD), zeros(POOL_BUF, B_WIDTH),
        zeros(CONV_W - 1, 3 * C_WIDTH), zeros(C_HEADS, C_HEAD, C_HEAD),
        wts, bp, sp, 0, False, tm=min(512, bp * sp), nb=1, lt=lt_p, c=c_p, nb_mem=1, lt_mem=min(sp, 512))
    nb_s = min(bs, 32)
    sample = _trunk(
        x_sample.reshape(bs * ss, D_MODEL),
        cache_mem_k.reshape(DEPTH, bs, n_mem, M_WIDTH), cache_mem_v.reshape(DEPTH, bs, n_mem, M_WIDTH),
        state_rwkv_shift.reshape(DEPTH, bs, 1, A_COLS), state_rwkv, state_pool, state_delta_conv, state_delta,
        wts, bs, ss, PAST_LEN, True, tm=min(512, bs * ss), nb=nb_s, lt=ss, c=ss, nb_mem=min(bs, 16), lt_mem=ss)

    y_p, p_rwkv, p_shift, p_pool, p_delta, p_conv = prompt
    y_s, s_rwkv, s_shift, s_pool, s_delta, s_conv = sample
    return (y_p.reshape(bp, sp, D_MODEL), y_s.reshape(bs, ss, D_MODEL),
            p_rwkv, p_shift, p_pool, p_delta, p_conv,
            mem_k.reshape(DEPTH, bp, n_mem, M_HEADS, M_HEAD), mem_v.reshape(DEPTH, bp, n_mem, M_HEADS, M_HEAD),
            s_rwkv, s_shift, s_pool, s_delta, s_conv)
```

```python
import functools
import math

import jax
import jax.numpy as jnp
from jax import lax
from jax.experimental import pallas as pl
from jax.experimental.pallas import tpu as pltpu

F32 = jnp.float32
BF16 = jnp.bfloat16

D_MODEL = 1024
DEPTH = 4
PAST_LEN = 16384
A_WIDTH = 512
A_HEAD = 64
A_HEADS = 8
A_LORA = 128
A_G_RANK = 128
A_COLS = 3 * A_WIDTH + A_LORA + A_G_RANK
GN_EPS = 64e-5
B_WIDTH = 512
B_GROUP = 128
POOL_WINDOWS = (2, 4, 8, 16)
POOL_BUF = 15
C_WIDTH = 512
C_HEAD = 128
C_HEADS = 4
CONV_W = 4
C_COLS = 4 * C_WIDTH + 2 * C_HEADS
C_COLS_PAD = 4 * C_WIDTH + 128
N_MEM = 256
M_HEADS = 4
M_HEAD = 64
M_WIDTH = 256
N_BRANCH = 4
D_FF = 2048
FF_CHUNK = 512
ALPHA = (2.0 * DEPTH) ** 0.25
LN_EPS = 1e-5
EXP_M05 = math.exp(-0.5)
ITEMS_PER_STEP = 16

OFF_A = 0
OFF_B = OFF_A + A_COLS
OFF_C = OFF_B + B_WIDTH
OFF_M = OFF_C + C_COLS
OFF_G = OFF_M + M_WIDTH

VMEM_LIMIT = 56 * 1024 * 1024


def _dot(a, b):
    return jnp.dot(a.astype(BF16), b.astype(BF16), preferred_element_type=F32)


def _dot_nt(a, b):
    return lax.dot_general(a.astype(BF16), b.astype(BF16), (((1,), (1,)), ((), ())),
                           preferred_element_type=F32)


def _dot_tn(a, b):
    return lax.dot_general(a.astype(BF16), b.astype(BF16), (((0,), (0,)), ((), ())),
                           preferred_element_type=F32)


def _bf16_pieces(x, n):
    pieces = []
    rest = x
    for _ in range(n):
        p = rest.astype(BF16)
        pieces.append(p)
        rest = rest - p.astype(F32)
    return pieces


def _dot_exact_lhs(m, x):
    out = None
    for p in _bf16_pieces(x, 3):
        d = jnp.dot(m, p, preferred_element_type=F32)
        out = d if out is None else out + d
    return out


def _dot_exact_rhs(x, m):
    out = None
    for p in _bf16_pieces(x, 2):
        d = jnp.dot(p, m, preferred_element_type=F32)
        out = d if out is None else out + d
    return out


def _layer_norm(z, g, b):
    mu = jnp.mean(z, axis=-1, keepdims=True)
    zc = z - mu
    var = jnp.mean(zc * zc, axis=-1, keepdims=True)
    return zc * lax.rsqrt(var + LN_EPS) * g + b


def _sigmoid(x):
    return 1.0 / (1.0 + jnp.exp(-x))


def _silu(x):
    return x * _sigmoid(x)


def _softplus(x):
    return jnp.maximum(x, 0.0) + jnp.log(1.0 + jnp.exp(-jnp.abs(x)))


def _tri_masks(c):
    row = lax.broadcasted_iota(jnp.int32, (c, c), 0)
    col = lax.broadcasted_iota(jnp.int32, (c, c), 1)
    return row > col, row >= col, (row == col).astype(F32)


def _segment_cumsum_mask(rows, c):
    shift = c.bit_length() - 1
    row = lax.broadcasted_iota(jnp.int32, (rows, rows), 0)
    col = lax.broadcasted_iota(jnp.int32, (rows, rows), 1)
    same = (row >> shift) == (col >> shift)
    return jnp.where(same & (col <= row), 1.0, 0.0).astype(BF16)


def _tri_inv_many(mats, eye, c):
    t = [eye + a for a in mats]
    p = mats
    k = 2
    while k < c:
        p = [_dot(x, x) for x in p]
        t = [t_ + _dot(t_, p_) for t_, p_ in zip(t, p)]
        k *= 2
    return t


def _ffn_kernel(x_ref, win_ref, wout_ref, g_ref, b_ref, o_ref):
    x = x_ref[...]
    xb = x.astype(BF16)
    acc = None
    for j in range(D_FF // FF_CHUNK):
        lo = j * FF_CHUNK
        gate = jnp.dot(xb, win_ref[:, lo:lo + FF_CHUNK], preferred_element_type=F32)
        up = jnp.dot(xb, win_ref[:, D_FF + lo:D_FF + lo + FF_CHUNK], preferred_element_type=F32)
        part = _dot(_silu(gate) * up, wout_ref[lo:lo + FF_CHUNK, :])
        acc = part if acc is None else acc + part
    o_ref[...] = _layer_norm(ALPHA * x + 0.5 * acc, g_ref[...], b_ref[...])


def _ffn(x, w_in, w_out, g, b, layer, idx, tm):
    rows = x.shape[0]
    const = lambda i: (layer, 0, 0)
    return pl.pallas_call(
        _ffn_kernel,
        grid=(rows // tm,),
        in_specs=[
            pl.BlockSpec((tm, D_MODEL), lambda i: (i, 0)),
            pl.BlockSpec((None, D_MODEL, 2 * D_FF), const),
            pl.BlockSpec((None, D_FF, D_MODEL), const),
            pl.BlockSpec((None, 1, D_MODEL), lambda i: (layer * 3 + idx, 0, 0)),
            pl.BlockSpec((None, 1, D_MODEL), lambda i: (layer * 3 + idx, 0, 0)),
        ],
        out_specs=pl.BlockSpec((tm, D_MODEL), lambda i: (i, 0)),
        out_shape=jax.ShapeDtypeStruct((rows, D_MODEL), F32),
        compiler_params=pltpu.CompilerParams(
            dimension_semantics=("parallel",), vmem_limit_bytes=VMEM_LIMIT),
        name="ffn",
    )(x, w_in, w_out, g, b)


def _merge_kernel(x_ref, oa_ref, ob_ref, oc_ref, om_ref, wg_ref, wba_ref, wbb_ref, wbc_ref, wbm_ref,
                  wo_ref, g_ref, b_ref, o_ref):
    x = x_ref[...]
    xb = x.astype(BF16)
    merged = None
    for i, (br_ref, wb_ref) in enumerate(
            ((oa_ref, wba_ref), (ob_ref, wbb_ref), (oc_ref, wbc_ref), (om_ref, wbm_ref))):
        gate = _sigmoid(jnp.dot(xb, wg_ref[:, i * D_MODEL:(i + 1) * D_MODEL], preferred_element_type=F32))
        term = gate * jnp.dot(br_ref[...], wb_ref[...], preferred_element_type=F32)
        merged = term if merged is None else merged + term
    mix = _dot(merged, wo_ref[...])
    o_ref[...] = _layer_norm(ALPHA * x + mix, g_ref[...], b_ref[...])


def _merge(x, oa, ob, oc, om, wts, layer, tm):
    rows = x.shape[0]
    const = lambda i: (layer, 0, 0)
    row_spec = lambda w: pl.BlockSpec((tm, w), lambda i: (i, 0))
    return pl.pallas_call(
        _merge_kernel,
        grid=(rows // tm,),
        in_specs=[
            row_spec(D_MODEL), row_spec(A_WIDTH), row_spec(B_WIDTH), row_spec(C_WIDTH), row_spec(M_WIDTH),
            pl.BlockSpec((None, D_MODEL, N_BRANCH * D_MODEL), const),
            pl.BlockSpec((None, A_WIDTH, D_MODEL), const),
            pl.BlockSpec((None, B_WIDTH, D_MODEL), const),
            pl.BlockSpec((None, C_WIDTH, D_MODEL), const),
            pl.BlockSpec((None, M_WIDTH, D_MODEL), const),
            pl.BlockSpec((None, D_MODEL, D_MODEL), const),
            pl.BlockSpec((None, 1, D_MODEL), lambda i: (layer * 3 + 1, 0, 0)),
            pl.BlockSpec((None, 1, D_MODEL), lambda i: (layer * 3 + 1, 0, 0)),
        ],
        out_specs=row_spec(D_MODEL),
        out_shape=jax.ShapeDtypeStruct((rows, D_MODEL), F32),
        compiler_params=pltpu.CompilerParams(
            dimension_semantics=("parallel",), vmem_limit_bytes=VMEM_LIMIT),
        name="merge",
    )(x, oa, ob, oc, om, wts["w_g"], wts["wb_a"], wts["wb_b"], wts["wb_c"], wts["wb_m"], wts["w_out"],
      wts["ln_g"], wts["ln_b"])


def _memkv_kernel(m_ref, w_ref, k_ref, v_ref):
    kv = _dot(m_ref[...], w_ref[...])
    k_ref[...] = kv[:, :M_WIDTH]
    v_ref[...] = kv[:, M_WIDTH:]


def _memkv(mem, w_kv):
    rows = mem.shape[0]
    out = jax.ShapeDtypeStruct((DEPTH, rows, M_WIDTH), F32)
    return pl.pallas_call(
        _memkv_kernel,
        grid=(DEPTH,),
        in_specs=[
            pl.BlockSpec((rows, D_MODEL), lambda l: (0, 0)),
            pl.BlockSpec((None, D_MODEL, 2 * M_WIDTH), lambda l: (l, 0, 0)),
        ],
        out_specs=[pl.BlockSpec((None, rows, M_WIDTH), lambda l: (l, 0, 0))] * 2,
        out_shape=[out, out],
        compiler_params=pltpu.CompilerParams(
            dimension_semantics=("parallel",), vmem_limit_bytes=VMEM_LIMIT),
        name="memkv",
    )(mem, w_kv)


def _rwkv_kernel(x_ref, shift_ref, s0_ref, wa_ref, mu_ref, w0_ref, wup_ref, a0_ref, aup_ref, gup_ref,
                 kk_ref, ka_ref, rk_ref, gnw_ref, gnb_ref, e_ref,
                 o_ref, shift_out_ref, s_out_ref,
                 hist_ref, rt_ref, kt_ref, at_ref, bt_ref, v_ref, ec_ref, y_ref, ahat_ref, uv_ref, yv_ref, arb_ref,
                 *, nb, lt, c, gp, gs):
    t = pl.program_id(1)
    rows = nb * lt
    cps = lt // c

    @pl.when(t == 0)
    def _():
        hist_ref[:, 7:8, :] = shift_ref[...]
        s_out_ref[...] = s0_ref[...]

    pa = _dot(x_ref[...], wa_ref[...])
    hist_ref[:, 8:8 + lt, :] = pa.reshape(nb, lt, A_COLS)
    prev = hist_ref[:, 7:7 + lt, :].reshape(rows, A_COLS)
    last = hist_ref[:, 7 + lt:8 + lt, :]
    shift_out_ref[...] = last
    hist_ref[:, 7:8, :] = last

    xm = pa + (prev - pa) * mu_ref[...]
    r = xm[:, 0:A_WIDTH]
    k = xm[:, A_WIDTH:2 * A_WIDTH]
    v = xm[:, 2 * A_WIDTH:3 * A_WIDTH]
    xwa = xm[:, 3 * A_WIDTH:3 * A_WIDTH + A_LORA]
    xg = xm[:, 3 * A_WIDTH + A_LORA:]
    z = w0_ref[...] + _dot(jnp.tanh(xwa), wup_ref[...])
    lw = -EXP_M05 * _sigmoid(z)
    a = _sigmoid(a0_ref[...] + _dot(xwa, aup_ref[...]))
    g = _dot(_sigmoid(xg), gup_ref[...])
    e = e_ref[...]
    kkr = k * kk_ref[...]
    kk = kkr * lax.rsqrt(_dot_exact_rhs(kkr * kkr, e) + 1e-12)
    k2 = k * (1.0 + (a - 1.0) * ka_ref[...])
    cum = _dot_exact_lhs(_segment_cumsum_mask(rows, c), lw)
    ecum = jnp.exp(cum)
    encum = jnp.exp(-cum)
    rt_ref[...] = r * ecum
    kt_ref[...] = k2 * encum
    bt_ref[...] = kk * a * encum
    at_ref[...] = -kk * jnp.exp(cum - lw)
    v_ref[...] = v
    ec_ref[...] = ecum

    strict, incl, eye = _tri_masks(c)
    n_chunks = rows // c
    heads = [slice(h * A_HEAD, (h + 1) * A_HEAD) for h in range(A_HEADS)]

    def state_free(gi, carry):
        pos, rt, kt, at, bt, vv = [], [], [], [], [], []
        for j in range(gp):
            rs = pl.ds(pl.multiple_of((gi * gp + j) * c, c), c)
            tiles = [ref[rs, :] for ref in (rt_ref, kt_ref, at_ref, bt_ref, v_ref)]
            for h, sl in enumerate(heads):
                pos.append((rs, h, sl))
                for dst, tile in zip((rt, kt, at, bt, vv), tiles):
                    dst.append(tile[:, sl])
        ar = [jnp.concatenate([a_, r_], axis=0) for a_, r_ in zip(at, rt)]
        gb = [_dot_nt(x, b_) for x, b_ in zip(ar, bt)]
        gk = [_dot_nt(x, k_) for x, k_ in zip(ar, kt)]
        a_ab = [jnp.where(strict, x[:c], 0.0) for x in gb]
        tinv = _tri_inv_many(a_ab, eye, c)
        akv = [_dot(jnp.where(strict, x[:c], 0.0), v_) for x, v_ in zip(gk, vv)]
        ahat = [_dot(t_, a_) for t_, a_ in zip(tinv, at)]
        uv = [_dot(t_, x) for t_, x in zip(tinv, akv)]
        yv = [_dot(jnp.where(incl, x[c:], 0.0), v_) for x, v_ in zip(gk, vv)]
        for (rs, h, sl), ah_, uv_, yv_, gb_ in zip(pos, ahat, uv, yv, gb):
            ahat_ref[rs, sl] = ah_
            uv_ref[rs, sl] = uv_
            yv_ref[rs, sl] = yv_
            arb_ref[rs, h * c:(h + 1) * c] = jnp.where(incl, gb_[c:], 0.0)
        return carry

    lax.fori_loop(0, n_chunks // gp, state_free, 0)

    def state_step(gi, carry):
        pos, ahat, rt, uv, yv, arb, vv, bk, wend, st = [], [], [], [], [], [], [], [], [], []
        for j in range(gs):
            i = gi * gs + j
            row0 = pl.multiple_of(i * c, c)
            rs = pl.ds(row0, c)
            b = i // cps
            tiles = [ref[rs, :] for ref in (ahat_ref, rt_ref, uv_ref, yv_ref, v_ref, bt_ref, kt_ref)]
            arb_c = arb_ref[rs, :]
            w_end = ec_ref[pl.ds(row0 + c - 1, 1), :]
            for h, sl in enumerate(heads):
                pos.append((rs, b, h, sl))
                ahat.append(tiles[0][:, sl])
                rt.append(tiles[1][:, sl])
                uv.append(tiles[2][:, sl])
                yv.append(tiles[3][:, sl])
                vv.append(tiles[4][:, sl])
                bk.append(jnp.concatenate([tiles[5][:, sl], tiles[6][:, sl]], axis=0))
                arb.append(arb_c[:, h * c:(h + 1) * c])
                wend.append(w_end[:, sl])
                st.append(s_out_ref[b, h])
        ps = [_dot_nt(jnp.concatenate([a_, r_], axis=0), s_) for a_, r_, s_ in zip(ahat, rt, st)]
        u = [p_[:c] + x for p_, x in zip(ps, uv)]
        y = [p_[c:] + _dot(m_, u_) + x for p_, m_, u_, x in zip(ps, arb, u, yv)]
        s_new = [(s_ + _dot_tn(jnp.concatenate([u_, v_], axis=0), bk_)) * w_
                 for s_, u_, v_, bk_, w_ in zip(st, u, vv, bk, wend)]
        for (rs, b, h, sl), y_, s_ in zip(pos, y, s_new):
            y_ref[rs, sl] = y_
            s_out_ref[b, h] = s_
        return carry

    lax.fori_loop(0, n_chunks // gs, state_step, 0)

    y = y_ref[...]
    inv_n = 1.0 / A_HEAD
    mean = _dot_exact_rhs(y, e) * inv_n
    yc = y - mean
    var = _dot_exact_rhs(yc * yc, e) * inv_n
    yn = yc * lax.rsqrt(var + GN_EPS) * gnw_ref[...] + gnb_ref[...]
    bonus = _dot_exact_rhs(r * k2 * rk_ref[...], e) * v
    o_ref[...] = ((yn + bonus) * g).astype(o_ref.dtype)


def _rwkv(x, shift, s0, wts, layer, state_layer, bsz, seq, nb, lt, c):
    rows = nb * lt
    tpb = seq // lt
    n_chunks = rows // c
    gp = min(n_chunks, ITEMS_PER_STEP // A_HEADS)
    gs = gp if lt == c else 1
    wl = lambda i, t: (layer, 0, 0)
    vec = lambda width: pl.BlockSpec((None, 1, width), wl)
    kern = functools.partial(_rwkv_kernel, nb=nb, lt=lt, c=c, gp=gp, gs=gs)
    return pl.pallas_call(
        kern,
        grid=(bsz // nb, tpb),
        in_specs=[
            pl.BlockSpec((rows, D_MODEL), lambda i, t: (i * tpb + t, 0)),
            pl.BlockSpec((None, nb, 1, A_COLS), lambda i, t: (state_layer, i, 0, 0)),
            pl.BlockSpec((None, nb, A_HEADS, A_HEAD, A_HEAD), lambda i, t: (state_layer, i, 0, 0, 0)),
            pl.BlockSpec((None, D_MODEL, A_COLS), wl),
            vec(A_COLS), vec(A_WIDTH),
            pl.BlockSpec((None, A_LORA, A_WIDTH), wl),
            vec(A_WIDTH),
            pl.BlockSpec((None, A_LORA, A_WIDTH), wl),
            pl.BlockSpec((None, A_G_RANK, A_WIDTH), wl),
            vec(A_WIDTH), vec(A_WIDTH), vec(A_WIDTH), vec(A_WIDTH), vec(A_WIDTH),
            pl.BlockSpec((A_WIDTH, A_WIDTH), lambda i, t: (0, 0)),
        ],
        out_specs=[
            pl.BlockSpec((rows, A_WIDTH), lambda i, t: (i * tpb + t, 0)),
            pl.BlockSpec((nb, 1, A_COLS), lambda i, t: (i, 0, 0)),
            pl.BlockSpec((nb, A_HEADS, A_HEAD, A_HEAD), lambda i, t: (i, 0, 0, 0)),
        ],
        out_shape=[
            jax.ShapeDtypeStruct((bsz * seq, A_WIDTH), BF16),
            jax.ShapeDtypeStruct((bsz, 1, A_COLS), F32),
            jax.ShapeDtypeStruct((bsz, A_HEADS, A_HEAD, A_HEAD), F32),
        ],
        scratch_shapes=[pltpu.VMEM((nb, 8 + lt, A_COLS), F32)]
        + [pltpu.VMEM((rows, A_WIDTH), F32)] * 10
        + [pltpu.VMEM((rows, A_HEADS * c), F32)],
        compiler_params=pltpu.CompilerParams(
            dimension_semantics=("parallel", "arbitrary"), vmem_limit_bytes=VMEM_LIMIT),
        name="rwkv",
    )(x, shift, s0, wts["w_a"], wts["rwkv_mu"], wts["rwkv_w0"], wts["rwkv_w_up"], wts["rwkv_a0"],
      wts["rwkv_a_up"], wts["rwkv_g_up"], wts["rwkv_k_k"], wts["rwkv_k_a"], wts["rwkv_r_k"],
      wts["rwkv_gn_w"], wts["rwkv_gn_b"], wts["head_ones"])


def _pool_kernel(x_ref, buf_ref, wb_ref, pw_ref, ps_ref, o_ref, buf_out_ref, hist_ref, *, nb, lt, pos0):
    t = pl.program_id(1)
    rows = nb * lt

    @pl.when(t == 0)
    def _():
        hist_ref[:, 1:16, :] = buf_ref[...]

    pb = _dot(x_ref[...], wb_ref[...])
    hist_ref[:, 16:16 + lt, :] = pb.reshape(nb, lt, B_WIDTH)
    pos = pos0 + t * lt + lax.broadcasted_iota(jnp.int32, (1, lt, 1), 1)
    parts = []
    for j, w in enumerate(POOL_WINDOWS):
        cs = slice(j * B_GROUP, (j + 1) * B_GROUP)
        cur = hist_ref[:, 16:16 + lt, cs]
        acc = cur
        for d in range(1, w):
            acc = acc + hist_ref[:, 16 - d:16 - d + lt, cs]
        count = jnp.minimum(pos + 1, w).astype(F32)
        parts.append(acc / count - cur)
    pooled = jnp.concatenate(parts, axis=-1).reshape(rows, B_WIDTH)
    o_ref[...] = (_dot(pooled, pw_ref[...]) * ps_ref[...]).astype(o_ref.dtype)
    tail = hist_ref[:, lt:lt + 16, :]
    buf_out_ref[...] = tail[:, 1:16, :]
    hist_ref[:, 0:16, :] = tail


def _pool(x, buf, wts, layer, state_layer, bsz, seq, nb, lt, pos0):
    rows = nb * lt
    tpb = seq // lt
    wl = lambda i, t: (layer, 0, 0)
    kern = functools.partial(_pool_kernel, nb=nb, lt=lt, pos0=pos0)
    return pl.pallas_call(
        kern,
        grid=(bsz // nb, tpb),
        in_specs=[
            pl.BlockSpec((rows, D_MODEL), lambda i, t: (i * tpb + t, 0)),
            pl.BlockSpec((None, nb, POOL_BUF, B_WIDTH), lambda i, t: (state_layer, i, 0, 0)),
            pl.BlockSpec((None, D_MODEL, B_WIDTH), wl),
            pl.BlockSpec((None, B_WIDTH, B_WIDTH), wl),
            pl.BlockSpec((None, 1, B_WIDTH), wl),
        ],
        out_specs=[
            pl.BlockSpec((rows, B_WIDTH), lambda i, t: (i * tpb + t, 0)),
            pl.BlockSpec((nb, POOL_BUF, B_WIDTH), lambda i, t: (i, 0, 0)),
        ],
        out_shape=[
            jax.ShapeDtypeStruct((bsz * seq, B_WIDTH), BF16),
            jax.ShapeDtypeStruct((bsz, POOL_BUF, B_WIDTH), F32),
        ],
        scratch_shapes=[pltpu.VMEM((nb, 16 + lt, B_WIDTH), F32)],
        compiler_params=pltpu.CompilerParams(
            dimension_semantics=("parallel", "arbitrary"), vmem_limit_bytes=VMEM_LIMIT),
        name="pool",
    )(x, buf, wts["w_b"], wts["pool_w"], wts["pool_scale"])


def _delta_kernel(x_ref, conv_ref, s0_ref, wc_ref, cw_ref, alog_ref, dtb_ref, nw_ref,
                  o_ref, conv_out_ref, s_out_ref,
                  hist_ref, q_ref, k_ref, v_ref, z_ref, beta_ref, cum_ref, y_ref,
                  u_ref, w_ref, qd_ref, kd_ref, aqk_ref, *, nb, lt, c, gp, gs):
    t = pl.program_id(1)
    rows = nb * lt
    cps = lt // c
    qkv_w = 3 * C_WIDTH

    @pl.when(t == 0)
    def _():
        hist_ref[:, 5:8, :] = conv_ref[...]
        s_out_ref[...] = s0_ref[...]

    pc = _dot(x_ref[...], wc_ref[...])
    hist_ref[:, 8:8 + lt, :] = pc[:, :qkv_w].reshape(nb, lt, qkv_w)
    z_ref[...] = pc[:, qkv_w:qkv_w + C_WIDTH]
    ba = pc[:, qkv_w + C_WIDTH:]
    conv = None
    for j in range(CONV_W):
        term = hist_ref[:, 5 + j:5 + j + lt, :] * cw_ref[j:j + 1, :]
        conv = term if conv is None else conv + term
    tail = hist_ref[:, 5 + lt:8 + lt, :]
    conv_out_ref[...] = tail
    hist_ref[:, 5:8, :] = tail
    act = _silu(conv).reshape(rows, qkv_w)
    for h in range(C_HEADS):
        qs = slice(h * C_HEAD, (h + 1) * C_HEAD)
        ks = slice(C_WIDTH + h * C_HEAD, C_WIDTH + (h + 1) * C_HEAD)
        qh = act[:, qs]
        kh = act[:, ks]
        q_ref[:, qs] = qh * (lax.rsqrt(jnp.sum(qh * qh, axis=-1, keepdims=True) + 1e-12) * C_HEAD ** -0.5)
        k_ref[:, qs] = kh * lax.rsqrt(jnp.sum(kh * kh, axis=-1, keepdims=True) + 1e-12)
    v_ref[...] = act[:, 2 * C_WIDTH:]
    beta_ref[...] = _sigmoid(ba)
    g = -jnp.exp(alog_ref[...]) * _softplus(ba + dtb_ref[...])
    cum_ref[...] = _dot_exact_lhs(_segment_cumsum_mask(rows, c), g)

    strict, incl, eye = _tri_masks(c)
    n_chunks = rows // c
    heads = [slice(h * C_HEAD, (h + 1) * C_HEAD) for h in range(C_HEADS)]

    def state_free(gi, carry):
        pos, qh, kh, vh, bh, col, rowv, g_end = [], [], [], [], [], [], [], []
        for j in range(gp):
            row0 = pl.multiple_of((gi * gp + j) * c, c)
            rs = pl.ds(row0, c)
            q_c, k_c, v_c = q_ref[rs, :], k_ref[rs, :], v_ref[rs, :]
            beta_c = beta_ref[rs, :]
            cum_c = cum_ref[rs, :]
            cum_t = cum_c.T
            cum_end = cum_ref[pl.ds(row0 + c - 1, 1), :]
            for h, sl in enumerate(heads):
                pos.append((rs, h, sl))
                qh.append(q_c[:, sl])
                kh.append(k_c[:, sl])
                vh.append(v_c[:, sl])
                bh.append(beta_c[:, h:h + 1])
                col.append(cum_c[:, C_HEADS + h:C_HEADS + h + 1])
                rowv.append(cum_t[C_HEADS + h:C_HEADS + h + 1, :])
                g_end.append(cum_end[:, C_HEADS + h:C_HEADS + h + 1])
        decay = [jnp.where(incl, jnp.exp(jnp.where(incl, c_ - r_, 0.0)), 0.0) for c_, r_ in zip(col, rowv)]
        kb = [k_ * b_ for k_, b_ in zip(kh, bh)]
        gm = [_dot_nt(jnp.concatenate([kb_, q_], axis=0), k_) for kb_, q_, k_ in zip(kb, qh, kh)]
        tinv = _tri_inv_many([jnp.where(strict, -(g_[:c] * d_), 0.0) for g_, d_ in zip(gm, decay)], eye, c)
        ecol = [jnp.exp(c_) for c_ in col]
        u = [_dot(t_, v_ * b_) for t_, v_, b_ in zip(tinv, vh, bh)]
        w = [_dot(t_, kb_ * e_) for t_, kb_, e_ in zip(tinv, kb, ecol)]
        for i, (rs, h, sl) in enumerate(pos):
            u_ref[rs, sl] = u[i]
            w_ref[rs, sl] = w[i]
            qd_ref[rs, sl] = qh[i] * ecol[i]
            kd_ref[rs, sl] = kh[i] * jnp.exp(g_end[i] - col[i])
            aqk_ref[rs, h * c:(h + 1) * c] = gm[i][c:] * decay[i]
        return carry

    lax.fori_loop(0, n_chunks // gp, state_free, 0)

    def state_step(gi, carry):
        pos, wq, u, aqk, kd, decay_end, st = [], [], [], [], [], [], []
        for j in range(gs):
            i = gi * gs + j
            row0 = pl.multiple_of(i * c, c)
            rs = pl.ds(row0, c)
            b = i // cps
            w_c, qd_c, u_c, kd_c, aqk_c = w_ref[rs, :], qd_ref[rs, :], u_ref[rs, :], kd_ref[rs, :], aqk_ref[rs, :]
            e_end = jnp.exp(cum_ref[pl.ds(row0 + c - 1, 1), :])
            for h, sl in enumerate(heads):
                pos.append((rs, b, h, sl))
                wq.append(jnp.concatenate([w_c[:, sl], qd_c[:, sl]], axis=0))
                u.append(u_c[:, sl])
                kd.append(kd_c[:, sl])
                aqk.append(aqk_c[:, h * c:(h + 1) * c])
                decay_end.append(e_end[:, C_HEADS + h:C_HEADS + h + 1])
                st.append(s_out_ref[b, h])
        ps = [_dot(x, s_) for x, s_ in zip(wq, st)]
        v_new = [u_ - p_[:c] for u_, p_ in zip(u, ps)]
        y = [p_[c:] + _dot(a_, v_) for p_, a_, v_ in zip(ps, aqk, v_new)]
        s_new = [s_ * e_ + _dot_tn(k_, v_) for s_, e_, k_, v_ in zip(st, decay_end, kd, v_new)]
        for (rs, b, h, sl), y_, s_ in zip(pos, y, s_new):
            y_ref[rs, sl] = y_
            s_out_ref[b, h] = s_
        return carry

    lax.fori_loop(0, n_chunks // gs, state_step, 0)

    for h in range(C_HEADS):
        sl = slice(h * C_HEAD, (h + 1) * C_HEAD)
        oh = y_ref[:, sl]
        oh = oh * lax.rsqrt(jnp.mean(oh * oh, axis=-1, keepdims=True) + 1e-6) * nw_ref[...]
        o_ref[:, sl] = (oh * _silu(z_ref[:, sl])).astype(o_ref.dtype)


def _delta(x, conv, s0, wts, layer, state_layer, bsz, seq, nb, lt, c):
    rows = nb * lt
    tpb = seq // lt
    wl = lambda i, t: (layer, 0, 0)
    qkv_w = 3 * C_WIDTH
    n_chunks = rows // c
    gp = min(n_chunks, ITEMS_PER_STEP // C_HEADS)
    gs = gp if lt == c else 1
    kern = functools.partial(_delta_kernel, nb=nb, lt=lt, c=c, gp=gp, gs=gs)
    return pl.pallas_call(
        kern,
        grid=(bsz // nb, tpb),
        in_specs=[
            pl.BlockSpec((rows, D_MODEL), lambda i, t: (i * tpb + t, 0)),
            pl.BlockSpec((None, nb, CONV_W - 1, qkv_w), lambda i, t: (state_layer, i, 0, 0)),
            pl.BlockSpec((None, nb, C_HEADS, C_HEAD, C_HEAD), lambda i, t: (state_layer, i, 0, 0, 0)),
            pl.BlockSpec((None, D_MODEL, C_COLS_PAD), wl),
            pl.BlockSpec((None, CONV_W, qkv_w), wl),
            pl.BlockSpec((None, 1, 128), wl),
            pl.BlockSpec((None, 1, 128), wl),
            pl.BlockSpec((None, 1, C_HEAD), wl),
        ],
        out_specs=[
            pl.BlockSpec((rows, C_WIDTH), lambda i, t: (i * tpb + t, 0)),
            pl.BlockSpec((nb, CONV_W - 1, qkv_w), lambda i, t: (i, 0, 0)),
            pl.BlockSpec((nb, C_HEADS, C_HEAD, C_HEAD), lambda i, t: (i, 0, 0, 0)),
        ],
        out_shape=[
            jax.ShapeDtypeStruct((bsz * seq, C_WIDTH), BF16),
            jax.ShapeDtypeStruct((bsz, CONV_W - 1, qkv_w), F32),
            jax.ShapeDtypeStruct((bsz, C_HEADS, C_HEAD, C_HEAD), F32),
        ],
        scratch_shapes=[pltpu.VMEM((nb, 8 + lt, qkv_w), F32)]
        + [pltpu.VMEM((rows, C_WIDTH), F32)] * 4
        + [pltpu.VMEM((rows, 128), F32)] * 2
        + [pltpu.VMEM((rows, C_WIDTH), F32)] * 5
        + [pltpu.VMEM((rows, C_HEADS * c), F32)],
        compiler_params=pltpu.CompilerParams(
            dimension_semantics=("parallel", "arbitrary"), vmem_limit_bytes=VMEM_LIMIT),
        name="delta",
    )(x, conv, s0, wts["w_c"], wts["delta_conv_w"], wts["delta_a_log"], wts["delta_dt_bias"],
      wts["delta_norm_w"])


def _mem_kernel(x_ref, k_ref, v_ref, wm_ref, o_ref, q_ref, *, nb, lt):
    q_ref[...] = _dot(x_ref[...], wm_ref[...]) * (M_HEAD ** -0.5)
    lane = lax.broadcasted_iota(jnp.int32, (1, M_WIDTH), 1)

    def seq_body(b, carry):
        rs = pl.ds(pl.multiple_of(b * lt, lt), lt)
        q = q_ref[rs, :]
        keys = k_ref[b].astype(BF16)
        vals = v_ref[b].astype(BF16)
        out = jnp.zeros((lt, M_WIDTH), F32)
        for h in range(M_HEADS):
            head = (lane >= h * M_HEAD) & (lane < (h + 1) * M_HEAD)
            s = _dot_nt(jnp.where(head, q, 0.0), keys)
            p = jnp.exp(s - jnp.max(s, axis=-1, keepdims=True))
            p = p / jnp.sum(p, axis=-1, keepdims=True)
            out = jnp.where(head, _dot(p, vals), out)
        o_ref[rs, :] = out.astype(o_ref.dtype)
        return carry

    lax.fori_loop(0, nb, seq_body, 0)


def _mem(x, mem_k, mem_v, wts, layer, bsz, seq, nb, lt):
    rows = nb * lt
    tpb = seq // lt
    kern = functools.partial(_mem_kernel, nb=nb, lt=lt)
    kv_spec = pl.BlockSpec((None, nb, N_MEM, M_WIDTH), lambda i, t: (layer, i, 0, 0))
    return pl.pallas_call(
        kern,
        grid=(bsz // nb, tpb),
        in_specs=[
            pl.BlockSpec((rows, D_MODEL), lambda i, t: (i * tpb + t, 0)),
            kv_spec, kv_spec,
            pl.BlockSpec((None, D_MODEL, M_WIDTH), lambda i, t: (layer, 0, 0)),
        ],
        out_specs=pl.BlockSpec((rows, M_WIDTH), lambda i, t: (i * tpb + t, 0)),
        out_shape=jax.ShapeDtypeStruct((bsz * seq, M_WIDTH), BF16),
        scratch_shapes=[pltpu.VMEM((rows, M_WIDTH), F32)],
        compiler_params=pltpu.CompilerParams(
            dimension_semantics=("parallel", "arbitrary"), vmem_limit_bytes=VMEM_LIMIT),
        name="mem",
    )(x, mem_k, mem_v, wts["w_m"])


def _prepare_weights(w_in, rwkv_mu, rwkv_w0, rwkv_w_up, rwkv_a0, rwkv_a_up, rwkv_g_up, rwkv_k_k, rwkv_k_a,
                     rwkv_r_k, rwkv_gn_w, rwkv_gn_b, pool_w, pool_scale, delta_conv_w, delta_a_log,
                     delta_dt_bias, delta_norm_w, w_branch, w_out, ffn1_w_in, ffn1_w_out, ffn2_w_in,
                     ffn2_w_out, ln_g, ln_b):
    depth = w_in.shape[0]
    row = lambda p: p.reshape(depth, 1, -1).astype(F32)
    zeros_lora = jnp.zeros((depth, A_LORA // 2, A_WIDTH), F32)
    lane_row = lambda p: jnp.pad(p.astype(F32), ((0, 0), (C_HEADS, 128 - 2 * C_HEADS))).reshape(depth, 1, 128)
    pool_bd = jnp.zeros((depth, B_WIDTH, B_WIDTH), F32)
    for gidx in range(len(POOL_WINDOWS)):
        sl = slice(gidx * B_GROUP, (gidx + 1) * B_GROUP)
        pool_bd = pool_bd.at[:, sl, sl].set(pool_w[:, gidx].astype(F32))
    head = jnp.arange(A_WIDTH) // A_HEAD
    w_c = jnp.pad(w_in[:, :, OFF_C:OFF_M], ((0, 0), (0, 0), (0, C_COLS_PAD - C_COLS)))
    return {
        "w_a": w_in[:, :, OFF_A:OFF_B].astype(BF16),
        "w_b": w_in[:, :, OFF_B:OFF_C].astype(BF16),
        "w_c": w_c.astype(BF16),
        "w_m": w_in[:, :, OFF_M:OFF_G].astype(BF16),
        "w_g": w_in[:, :, OFF_G:].astype(BF16),
        "rwkv_mu": row(rwkv_mu), "rwkv_w0": row(rwkv_w0), "rwkv_a0": row(rwkv_a0),
        "rwkv_w_up": jnp.concatenate([rwkv_w_up.astype(F32), zeros_lora], axis=1).astype(BF16),
        "rwkv_a_up": jnp.concatenate([zeros_lora, rwkv_a_up.astype(F32)], axis=1).astype(BF16),
        "rwkv_g_up": rwkv_g_up.astype(BF16),
        "rwkv_k_k": row(rwkv_k_k), "rwkv_k_a": row(rwkv_k_a), "rwkv_r_k": row(rwkv_r_k),
        "rwkv_gn_w": row(rwkv_gn_w), "rwkv_gn_b": row(rwkv_gn_b),
        "head_ones": (head[:, None] == head[None, :]).astype(BF16),
        "pool_w": pool_bd.astype(BF16), "pool_scale": row(pool_scale),
        "delta_conv_w": delta_conv_w.astype(F32),
        "delta_a_log": lane_row(delta_a_log), "delta_dt_bias": lane_row(delta_dt_bias),
        "delta_norm_w": row(delta_norm_w),
        "wb_a": w_branch[:, 0:A_WIDTH].astype(BF16),
        "wb_b": w_branch[:, A_WIDTH:A_WIDTH + B_WIDTH].astype(BF16),
        "wb_c": w_branch[:, A_WIDTH + B_WIDTH:A_WIDTH + B_WIDTH + C_WIDTH].astype(BF16),
        "wb_m": w_branch[:, A_WIDTH + B_WIDTH + C_WIDTH:].astype(BF16),
        "w_out": w_out.astype(BF16),
        "ffn1_w_in": ffn1_w_in.astype(BF16), "ffn1_w_out": ffn1_w_out.astype(BF16),
        "ffn2_w_in": ffn2_w_in.astype(BF16), "ffn2_w_out": ffn2_w_out.astype(BF16),
        "ln_g": ln_g.reshape(depth * 3, 1, D_MODEL).astype(F32),
        "ln_b": ln_b.reshape(depth * 3, 1, D_MODEL).astype(F32),
    }


def _trunk(x, mem_k, mem_v, shift, s_rwkv, pool_buf, conv_buf, s_delta, wts, bsz, seq, pos0, stacked,
           tm, nb, lt, c, nb_mem, lt_mem):
    new = []
    for l in range(DEPTH):
        sl = l if stacked else 0
        x = _ffn(x, wts["ffn1_w_in"], wts["ffn1_w_out"], wts["ln_g"], wts["ln_b"], l, 0, tm)
        oa, shift_n, rwkv_n = _rwkv(x, shift, s_rwkv, wts, l, sl, bsz, seq, nb, lt, c)
        ob, pool_n = _pool(x, pool_buf, wts, l, sl, bsz, seq, nb, lt, pos0)
        oc, conv_n, delta_n = _delta(x, conv_buf, s_delta, wts, l, sl, bsz, seq, nb, lt, c)
        om = _mem(x, mem_k, mem_v, wts, l, bsz, seq, nb_mem, lt_mem)
        x = _merge(x, oa, ob, oc, om, wts, l, tm)
        x = _ffn(x, wts["ffn2_w_in"], wts["ffn2_w_out"], wts["ln_g"], wts["ln_b"], l, 2, tm)
        new.append((rwkv_n, shift_n.reshape(bsz, A_COLS), pool_n, delta_n, conv_n))
    return (x,) + tuple(jnp.stack([st[i] for st in new]) for i in range(5))


def kernel(x_prompt, x_sample, mem_prompt, cache_mem_k, cache_mem_v, state_rwkv, state_rwkv_shift, state_pool, state_delta, state_delta_conv, w_in, rwkv_mu, rwkv_w0, rwkv_w_up, rwkv_a0, rwkv_a_up, rwkv_g_up, rwkv_k_k, rwkv_k_a, rwkv_r_k, rwkv_gn_w, rwkv_gn_b, pool_w, pool_scale, delta_conv_w, delta_a_log, delta_dt_bias, delta_norm_w, mem_w_kv, w_branch, w_out, ffn1_w_in, ffn1_w_out, ffn2_w_in, ffn2_w_out, ln_g, ln_b):
    wts = _prepare_weights(w_in, rwkv_mu, rwkv_w0, rwkv_w_up, rwkv_a0, rwkv_a_up, rwkv_g_up, rwkv_k_k,
                           rwkv_k_a, rwkv_r_k, rwkv_gn_w, rwkv_gn_b, pool_w, pool_scale, delta_conv_w,
                           delta_a_log, delta_dt_bias, delta_norm_w, w_branch, w_out, ffn1_w_in,
                           ffn1_w_out, ffn2_w_in, ffn2_w_out, ln_g, ln_b)
    bp, sp, _ = x_prompt.shape
    bs, ss, _ = x_sample.shape
    n_mem = mem_prompt.shape[1]

    mem_k, mem_v = _memkv(mem_prompt.reshape(bp * n_mem, D_MODEL), mem_w_kv.astype(BF16))
    mem_k = mem_k.reshape(DEPTH, bp, n_mem, M_WIDTH)
    mem_v = mem_v.reshape(DEPTH, bp, n_mem, M_WIDTH)

    zeros = lambda *shape: jnp.zeros((1, bp) + shape, F32)
    lt_p = min(sp, 256)
    c_p = min(lt_p, 64)
    prompt = _trunk(
        x_prompt.reshape(bp * sp, D_MODEL), mem_k, mem_v,
        zeros(1, A_COLS), zeros(A_HEADS, A_HEAD, A_HEAD), zeros(POOL_BUF, B_WIDTH),
        zeros(CONV_W - 1, 3 * C_WIDTH), zeros(C_HEADS, C_HEAD, C_HEAD),
        wts, bp, sp, 0, False, tm=min(512, bp * sp), nb=1, lt=lt_p, c=c_p, nb_mem=1, lt_mem=min(sp, 512))
    nb_s = min(bs, 32)
    sample = _trunk(
        x_sample.reshape(bs * ss, D_MODEL),
        cache_mem_k.reshape(DEPTH, bs, n_mem, M_WIDTH), cache_mem_v.reshape(DEPTH, bs, n_mem, M_WIDTH),
        state_rwkv_shift.reshape(DEPTH, bs, 1, A_COLS), state_rwkv, state_pool, state_delta_conv, state_delta,
        wts, bs, ss, PAST_LEN, True, tm=min(512, bs * ss), nb=nb_s, lt=ss, c=ss, nb_mem=min(bs, 16), lt_mem=ss)

    y_p, p_rwkv, p_shift, p_pool, p_delta, p_conv = prompt
    y_s, s_rwkv, s_shift, s_pool, s_delta, s_conv = sample
    return (y_p.reshape(bp, sp, D_MODEL), y_s.reshape(bs, ss, D_MODEL),
            p_rwkv, p_shift, p_pool, p_delta, p_conv,
            mem_k.reshape(DEPTH, bp, n_mem, M_HEADS, M_HEAD), mem_v.reshape(DEPTH, bp, n_mem, M_HEADS, M_HEAD),
            s_rwkv, s_shift, s_pool, s_delta, s_conv)
```

```python
import functools
import math

import jax
import jax.numpy as jnp
from jax import lax
from jax.experimental import pallas as pl
from jax.experimental.pallas import tpu as pltpu

F32 = jnp.float32
BF16 = jnp.bfloat16

D_MODEL = 1024
DEPTH = 4
PAST_LEN = 16384
A_WIDTH = 512
A_HEAD = 64
A_HEADS = 8
A_LORA = 128
A_G_RANK = 128
A_COLS = 3 * A_WIDTH + A_LORA + A_G_RANK
GN_EPS = 64e-5
B_WIDTH = 512
B_GROUP = 128
POOL_WINDOWS = (2, 4, 8, 16)
POOL_BUF = 15
C_WIDTH = 512
C_HEAD = 128
C_HEADS = 4
CONV_W = 4
C_COLS = 4 * C_WIDTH + 2 * C_HEADS
C_COLS_PAD = 4 * C_WIDTH + 128
N_MEM = 256
M_HEADS = 4
M_HEAD = 64
M_WIDTH = 256
N_BRANCH = 4
D_FF = 2048
FF_CHUNK = 512
ALPHA = (2.0 * DEPTH) ** 0.25
LN_EPS = 1e-5
EXP_M05 = math.exp(-0.5)
ITEMS_PER_STEP = 16

OFF_A = 0
OFF_B = OFF_A + A_COLS
OFF_C = OFF_B + B_WIDTH
OFF_M = OFF_C + C_COLS
OFF_G = OFF_M + M_WIDTH

VMEM_LIMIT = 56 * 1024 * 1024


def _dot(a, b):
    return jnp.dot(a.astype(BF16), b.astype(BF16), preferred_element_type=F32)


def _dot_nt(a, b):
    return lax.dot_general(a.astype(BF16), b.astype(BF16), (((1,), (1,)), ((), ())),
                           preferred_element_type=F32)


def _dot_tn(a, b):
    return lax.dot_general(a.astype(BF16), b.astype(BF16), (((0,), (0,)), ((), ())),
                           preferred_element_type=F32)


def _bf16_pieces(x, n):
    pieces = []
    rest = x
    for _ in range(n):
        p = rest.astype(BF16)
        pieces.append(p)
        rest = rest - p.astype(F32)
    return pieces


def _dot_exact_lhs(m, x):
    out = None
    for p in _bf16_pieces(x, 3):
        d = jnp.dot(m, p, preferred_element_type=F32)
        out = d if out is None else out + d
    return out


def _dot_exact_rhs(x, m, pieces):
    out = None
    for p in _bf16_pieces(x, pieces):
        d = jnp.dot(p, m, preferred_element_type=F32)
        out = d if out is None else out + d
    return out


def _layer_norm(z, g, b):
    mu = jnp.mean(z, axis=-1, keepdims=True)
    zc = z - mu
    var = jnp.mean(zc * zc, axis=-1, keepdims=True)
    return zc * lax.rsqrt(var + LN_EPS) * g + b


def _sigmoid(x):
    return 1.0 / (1.0 + jnp.exp(-x))


def _silu(x):
    return x * _sigmoid(x)


def _softplus(x):
    return jnp.maximum(x, 0.0) + jnp.log(1.0 + jnp.exp(-jnp.abs(x)))


def _tri_masks(c):
    row = lax.broadcasted_iota(jnp.int32, (c, c), 0)
    col = lax.broadcasted_iota(jnp.int32, (c, c), 1)
    return row > col, row >= col, (row == col).astype(F32)


def _segment_cumsum_mask(rows, c):
    shift = c.bit_length() - 1
    row = lax.broadcasted_iota(jnp.int32, (rows, rows), 0)
    col = lax.broadcasted_iota(jnp.int32, (rows, rows), 1)
    same = (row >> shift) == (col >> shift)
    return jnp.where(same & (col <= row), 1.0, 0.0).astype(BF16)


def _tri_inv_many(mats, eye, c):
    t = [eye + a for a in mats]
    if c <= 2:
        return t
    p = [_dot(a, a) for a in mats]
    k = 4
    while k < c:
        both = [_dot(jnp.concatenate([p_, t_], axis=0), p_) for p_, t_ in zip(p, t)]
        t = [t_ + b_[c:] for t_, b_ in zip(t, both)]
        p = [b_[:c] for b_ in both]
        k *= 2
    return [t_ + _dot(t_, p_) for t_, p_ in zip(t, p)]


def _ffn_kernel(x_ref, win_ref, wout_ref, g_ref, b_ref, o_ref):
    x = x_ref[...]
    xb = x.astype(BF16)
    acc = None
    for j in range(D_FF // FF_CHUNK):
        lo = j * FF_CHUNK
        gate = jnp.dot(xb, win_ref[:, lo:lo + FF_CHUNK], preferred_element_type=F32)
        up = jnp.dot(xb, win_ref[:, D_FF + lo:D_FF + lo + FF_CHUNK], preferred_element_type=F32)
        part = _dot(_silu(gate) * up, wout_ref[lo:lo + FF_CHUNK, :])
        acc = part if acc is None else acc + part
    o_ref[...] = _layer_norm(ALPHA * x + 0.5 * acc, g_ref[...], b_ref[...])


def _ffn(x, w_in, w_out, g, b, layer, idx, tm):
    rows = x.shape[0]
    const = lambda i: (layer, 0, 0)
    return pl.pallas_call(
        _ffn_kernel,
        grid=(rows // tm,),
        in_specs=[
            pl.BlockSpec((tm, D_MODEL), lambda i: (i, 0)),
            pl.BlockSpec((None, D_MODEL, 2 * D_FF), const),
            pl.BlockSpec((None, D_FF, D_MODEL), const),
            pl.BlockSpec((None, 1, D_MODEL), lambda i: (layer * 3 + idx, 0, 0)),
            pl.BlockSpec((None, 1, D_MODEL), lambda i: (layer * 3 + idx, 0, 0)),
        ],
        out_specs=pl.BlockSpec((tm, D_MODEL), lambda i: (i, 0)),
        out_shape=jax.ShapeDtypeStruct((rows, D_MODEL), F32),
        compiler_params=pltpu.CompilerParams(
            dimension_semantics=("parallel",), vmem_limit_bytes=VMEM_LIMIT),
        name="ffn",
    )(x, w_in, w_out, g, b)


def _merge_kernel(x_ref, oa_ref, ob_ref, oc_ref, om_ref, wg_ref, wba_ref, wbb_ref, wbc_ref, wbm_ref,
                  wo_ref, g_ref, b_ref, o_ref):
    x = x_ref[...]
    xb = x.astype(BF16)
    merged = None
    for i, (br_ref, wb_ref) in enumerate(
            ((oa_ref, wba_ref), (ob_ref, wbb_ref), (oc_ref, wbc_ref), (om_ref, wbm_ref))):
        gate = _sigmoid(jnp.dot(xb, wg_ref[:, i * D_MODEL:(i + 1) * D_MODEL], preferred_element_type=F32))
        term = gate * jnp.dot(br_ref[...], wb_ref[...], preferred_element_type=F32)
        merged = term if merged is None else merged + term
    mix = _dot(merged, wo_ref[...])
    o_ref[...] = _layer_norm(ALPHA * x + mix, g_ref[...], b_ref[...])


def _merge(x, oa, ob, oc, om, wts, layer, tm):
    rows = x.shape[0]
    const = lambda i: (layer, 0, 0)
    row_spec = lambda w: pl.BlockSpec((tm, w), lambda i: (i, 0))
    return pl.pallas_call(
        _merge_kernel,
        grid=(rows // tm,),
        in_specs=[
            row_spec(D_MODEL), row_spec(A_WIDTH), row_spec(B_WIDTH), row_spec(C_WIDTH), row_spec(M_WIDTH),
            pl.BlockSpec((None, D_MODEL, N_BRANCH * D_MODEL), const),
            pl.BlockSpec((None, A_WIDTH, D_MODEL), const),
            pl.BlockSpec((None, B_WIDTH, D_MODEL), const),
            pl.BlockSpec((None, C_WIDTH, D_MODEL), const),
            pl.BlockSpec((None, M_WIDTH, D_MODEL), const),
            pl.BlockSpec((None, D_MODEL, D_MODEL), const),
            pl.BlockSpec((None, 1, D_MODEL), lambda i: (layer * 3 + 1, 0, 0)),
            pl.BlockSpec((None, 1, D_MODEL), lambda i: (layer * 3 + 1, 0, 0)),
        ],
        out_specs=row_spec(D_MODEL),
        out_shape=jax.ShapeDtypeStruct((rows, D_MODEL), F32),
        compiler_params=pltpu.CompilerParams(
            dimension_semantics=("parallel",), vmem_limit_bytes=VMEM_LIMIT),
        name="merge",
    )(x, oa, ob, oc, om, wts["w_g"], wts["wb_a"], wts["wb_b"], wts["wb_c"], wts["wb_m"], wts["w_out"],
      wts["ln_g"], wts["ln_b"])


def _memkv_kernel(m_ref, w_ref, k_ref, v_ref):
    kv = _dot_nt(w_ref[...], m_ref[...])
    k_ref[...] = kv[:M_WIDTH]
    v_ref[...] = kv[M_WIDTH:]


def _memkv(mem, w_kv_t, bsz, n_mem):
    out = jax.ShapeDtypeStruct((DEPTH, bsz, M_WIDTH, n_mem), F32)
    return pl.pallas_call(
        _memkv_kernel,
        grid=(DEPTH, bsz),
        in_specs=[
            pl.BlockSpec((n_mem, D_MODEL), lambda l, b: (b, 0)),
            pl.BlockSpec((None, 2 * M_WIDTH, D_MODEL), lambda l, b: (l, 0, 0)),
        ],
        out_specs=[pl.BlockSpec((None, None, M_WIDTH, n_mem), lambda l, b: (l, b, 0, 0))] * 2,
        out_shape=[out, out],
        compiler_params=pltpu.CompilerParams(
            dimension_semantics=("parallel", "parallel"), vmem_limit_bytes=VMEM_LIMIT),
        name="memkv",
    )(mem, w_kv_t)


def _rwkv_kernel(x_ref, shift_ref, s0_ref, wa_ref, mu_ref, w0_ref, wup_ref, a0_ref, aup_ref, gup_ref,
                 kk_ref, ka_ref, rk_ref, gnw_ref, gnb_ref, e_ref,
                 o_ref, shift_out_ref, s_out_ref,
                 hist_ref, rt_ref, kt_ref, at_ref, bt_ref, v_ref, ec_ref, y_ref, ahat_ref, uv_ref, yv_ref, arb_ref,
                 *, nb, lt, c, gp, gs):
    t = pl.program_id(1)
    rows = nb * lt
    cps = lt // c

    @pl.when(t == 0)
    def _():
        hist_ref[:, 7:8, :] = shift_ref[...]
        s_out_ref[...] = s0_ref[...]

    pa = _dot(x_ref[...], wa_ref[...])
    hist_ref[:, 8:8 + lt, :] = pa.reshape(nb, lt, A_COLS)
    prev = hist_ref[:, 7:7 + lt, :].reshape(rows, A_COLS)
    last = hist_ref[:, 7 + lt:8 + lt, :]
    shift_out_ref[...] = last
    hist_ref[:, 7:8, :] = last

    xm = pa + (prev - pa) * mu_ref[...]
    r = xm[:, 0:A_WIDTH]
    k = xm[:, A_WIDTH:2 * A_WIDTH]
    v = xm[:, 2 * A_WIDTH:3 * A_WIDTH]
    xwa = xm[:, 3 * A_WIDTH:3 * A_WIDTH + A_LORA]
    xg = xm[:, 3 * A_WIDTH + A_LORA:]
    z = w0_ref[...] + _dot(jnp.tanh(xwa), wup_ref[...])
    lw = -EXP_M05 * _sigmoid(z)
    a = _sigmoid(a0_ref[...] + _dot(xwa, aup_ref[...]))
    g = _dot(_sigmoid(xg), gup_ref[...])
    e = e_ref[...]
    kkr = k * kk_ref[...]
    kk = kkr * lax.rsqrt(_dot_exact_rhs(kkr * kkr, e, 1) + 1e-12)
    k2 = k * (1.0 + (a - 1.0) * ka_ref[...])
    cum = _dot_exact_lhs(_segment_cumsum_mask(rows, c), lw)
    ecum = jnp.exp(cum)
    encum = jnp.exp(-cum)
    rt_ref[...] = r * ecum
    kt_ref[...] = k2 * encum
    bt_ref[...] = kk * a * encum
    at_ref[...] = -kk * jnp.exp(cum - lw)
    v_ref[...] = v
    ec_ref[...] = ecum

    strict, incl, eye = _tri_masks(c)
    row2 = lax.broadcasted_iota(jnp.int32, (2 * c, c), 0)
    col2 = lax.broadcasted_iota(jnp.int32, (2 * c, c), 1)
    strict_incl = jnp.where(row2 < c, row2, row2 - c + 1) > col2
    n_chunks = rows // c
    heads = [slice(h * A_HEAD, (h + 1) * A_HEAD) for h in range(A_HEADS)]

    def state_free(gi, carry):
        pos, rt, kt, at, bt, vv = [], [], [], [], [], []
        for j in range(gp):
            rs = pl.ds(pl.multiple_of((gi * gp + j) * c, c), c)
            tiles = [ref[rs, :] for ref in (rt_ref, kt_ref, at_ref, bt_ref, v_ref)]
            for h, sl in enumerate(heads):
                pos.append((rs, h, sl))
                for dst, tile in zip((rt, kt, at, bt, vv), tiles):
                    dst.append(tile[:, sl])
        ar = [jnp.concatenate([a_, r_], axis=0) for a_, r_ in zip(at, rt)]
        gb = [_dot_nt(x, b_) for x, b_ in zip(ar, bt)]
        gk = [_dot_nt(x, k_) for x, k_ in zip(ar, kt)]
        a_ab = [jnp.where(strict, x[:c], 0.0) for x in gb]
        tinv = _tri_inv_many(a_ab, eye, c)
        kv = [_dot(jnp.where(strict_incl, x, 0.0), v_) for x, v_ in zip(gk, vv)]
        ahat = [_dot(t_, a_) for t_, a_ in zip(tinv, at)]
        uv = [_dot(t_, x[:c]) for t_, x in zip(tinv, kv)]
        for (rs, h, sl), ah_, uv_, kv_, gb_ in zip(pos, ahat, uv, kv, gb):
            ahat_ref[rs, sl] = ah_
            uv_ref[rs, sl] = uv_
            yv_ref[rs, sl] = kv_[c:]
            arb_ref[rs, h * c:(h + 1) * c] = jnp.where(incl, gb_[c:], 0.0)
        return carry

    lax.fori_loop(0, n_chunks // gp, state_free, 0)

    def state_step(gi, carry):
        pos, ahat, rt, uv, yv, arb, vv, bk, wend, st = [], [], [], [], [], [], [], [], [], []
        for j in range(gs):
            i = gi * gs + j
            row0 = pl.multiple_of(i * c, c)
            rs = pl.ds(row0, c)
            b = i // cps
            tiles = [ref[rs, :] for ref in (ahat_ref, rt_ref, uv_ref, yv_ref, v_ref, bt_ref, kt_ref)]
            arb_c = arb_ref[rs, :]
            w_end = ec_ref[pl.ds(row0 + c - 1, 1), :]
            for h, sl in enumerate(heads):
                pos.append((rs, b, h, sl))
                ahat.append(tiles[0][:, sl])
                rt.append(tiles[1][:, sl])
                uv.append(tiles[2][:, sl])
                yv.append(tiles[3][:, sl])
                vv.append(tiles[4][:, sl])
                bk.append(jnp.concatenate([tiles[5][:, sl], tiles[6][:, sl]], axis=0))
                arb.append(arb_c[:, h * c:(h + 1) * c])
                wend.append(w_end[:, sl])
                st.append(s_out_ref[b, h])
        ps = [_dot_nt(jnp.concatenate([a_, r_], axis=0), s_) for a_, r_, s_ in zip(ahat, rt, st)]
        u = [p_[:c] + x for p_, x in zip(ps, uv)]
        y = [p_[c:] + _dot(m_, u_) + x for p_, m_, u_, x in zip(ps, arb, u, yv)]
        s_new = [(s_ + _dot_tn(jnp.concatenate([u_, v_], axis=0), bk_)) * w_
                 for s_, u_, v_, bk_, w_ in zip(st, u, vv, bk, wend)]
        for (rs, b, h, sl), y_, s_ in zip(pos, y, s_new):
            y_ref[rs, sl] = y_
            s_out_ref[b, h] = s_
        return carry

    lax.fori_loop(0, n_chunks // gs, state_step, 0)

    y = y_ref[...]
    inv_n = 1.0 / A_HEAD
    mean = _dot_exact_rhs(y, e, 2) * inv_n
    yc = y - mean
    var = _dot_exact_rhs(yc * yc, e, 1) * inv_n
    yn = yc * lax.rsqrt(var + GN_EPS) * gnw_ref[...] + gnb_ref[...]
    bonus = _dot_exact_rhs(r * k2 * rk_ref[...], e, 2) * v
    o_ref[...] = ((yn + bonus) * g).astype(o_ref.dtype)


def _rwkv(x, shift, s0, wts, layer, state_layer, bsz, seq, nb, lt, c):
    rows = nb * lt
    tpb = seq // lt
    n_chunks = rows // c
    gp = min(n_chunks, ITEMS_PER_STEP // A_HEADS)
    gs = gp if lt == c else 1
    wl = lambda i, t: (layer, 0, 0)
    vec = lambda width: pl.BlockSpec((None, 1, width), wl)
    kern = functools.partial(_rwkv_kernel, nb=nb, lt=lt, c=c, gp=gp, gs=gs)
    return pl.pallas_call(
        kern,
        grid=(bsz // nb, tpb),
        in_specs=[
            pl.BlockSpec((rows, D_MODEL), lambda i, t: (i * tpb + t, 0)),
            pl.BlockSpec((None, nb, 1, A_COLS), lambda i, t: (state_layer, i, 0, 0)),
            pl.BlockSpec((None, nb, A_HEADS, A_HEAD, A_HEAD), lambda i, t: (state_layer, i, 0, 0, 0)),
            pl.BlockSpec((None, D_MODEL, A_COLS), wl),
            vec(A_COLS), vec(A_WIDTH),
            pl.BlockSpec((None, A_LORA, A_WIDTH), wl),
            vec(A_WIDTH),
            pl.BlockSpec((None, A_LORA, A_WIDTH), wl),
            pl.BlockSpec((None, A_G_RANK, A_WIDTH), wl),
            vec(A_WIDTH), vec(A_WIDTH), vec(A_WIDTH), vec(A_WIDTH), vec(A_WIDTH),
            pl.BlockSpec((A_WIDTH, A_WIDTH), lambda i, t: (0, 0)),
        ],
        out_specs=[
            pl.BlockSpec((rows, A_WIDTH), lambda i, t: (i * tpb + t, 0)),
            pl.BlockSpec((nb, 1, A_COLS), lambda i, t: (i, 0, 0)),
            pl.BlockSpec((nb, A_HEADS, A_HEAD, A_HEAD), lambda i, t: (i, 0, 0, 0)),
        ],
        out_shape=[
            jax.ShapeDtypeStruct((bsz * seq, A_WIDTH), BF16),
            jax.ShapeDtypeStruct((bsz, 1, A_COLS), F32),
            jax.ShapeDtypeStruct((bsz, A_HEADS, A_HEAD, A_HEAD), F32),
        ],
        scratch_shapes=[pltpu.VMEM((nb, 8 + lt, A_COLS), F32)]
        + [pltpu.VMEM((rows, A_WIDTH), F32)] * 10
        + [pltpu.VMEM((rows, A_HEADS * c), F32)],
        compiler_params=pltpu.CompilerParams(
            dimension_semantics=("parallel", "arbitrary"), vmem_limit_bytes=VMEM_LIMIT),
        name="rwkv",
    )(x, shift, s0, wts["w_a"], wts["rwkv_mu"], wts["rwkv_w0"], wts["rwkv_w_up"], wts["rwkv_a0"],
      wts["rwkv_a_up"], wts["rwkv_g_up"], wts["rwkv_k_k"], wts["rwkv_k_a"], wts["rwkv_r_k"],
      wts["rwkv_gn_w"], wts["rwkv_gn_b"], wts["head_ones"])


def _pool_kernel(x_ref, buf_ref, wb_ref, pw_ref, ps_ref, o_ref, buf_out_ref, hist_ref, *, nb, lt, pos0):
    t = pl.program_id(1)
    rows = nb * lt

    @pl.when(t == 0)
    def _():
        hist_ref[:, 1:16, :] = buf_ref[...]

    pb = _dot(x_ref[...], wb_ref[...])
    hist_ref[:, 16:16 + lt, :] = pb.reshape(nb, lt, B_WIDTH)
    pos = pos0 + t * lt + lax.broadcasted_iota(jnp.int32, (1, lt, 1), 1)
    parts = []
    for j, w in enumerate(POOL_WINDOWS):
        cs = slice(j * B_GROUP, (j + 1) * B_GROUP)
        cur = hist_ref[:, 16:16 + lt, cs]
        acc = cur
        for d in range(1, w):
            acc = acc + hist_ref[:, 16 - d:16 - d + lt, cs]
        count = jnp.minimum(pos + 1, w).astype(F32)
        parts.append(acc / count - cur)
    pooled = jnp.concatenate(parts, axis=-1).reshape(rows, B_WIDTH)
    o_ref[...] = (_dot(pooled, pw_ref[...]) * ps_ref[...]).astype(o_ref.dtype)
    tail = hist_ref[:, lt:lt + 16, :]
    buf_out_ref[...] = tail[:, 1:16, :]
    hist_ref[:, 0:16, :] = tail


def _pool_parts(buf, wts, layer, state_layer, bsz, seq, nb, lt):
    rows = nb * lt
    tpb = seq // lt
    wl = lambda i, t: (layer, 0, 0)
    return dict(
        args=[buf, wts["w_b"], wts["pool_w"], wts["pool_scale"]],
        in_specs=[
            pl.BlockSpec((None, nb, POOL_BUF, B_WIDTH), lambda i, t: (state_layer, i, 0, 0)),
            pl.BlockSpec((None, D_MODEL, B_WIDTH), wl),
            pl.BlockSpec((None, B_WIDTH, B_WIDTH), wl),
            pl.BlockSpec((None, 1, B_WIDTH), wl),
        ],
        out_specs=[
            pl.BlockSpec((rows, B_WIDTH), lambda i, t: (i * tpb + t, 0)),
            pl.BlockSpec((nb, POOL_BUF, B_WIDTH), lambda i, t: (i, 0, 0)),
        ],
        out_shape=[
            jax.ShapeDtypeStruct((bsz * seq, B_WIDTH), BF16),
            jax.ShapeDtypeStruct((bsz, POOL_BUF, B_WIDTH), F32),
        ],
        scratch=[pltpu.VMEM((nb, 16 + lt, B_WIDTH), F32)],
    )


def _delta_kernel(x_ref, conv_ref, s0_ref, wc_ref, cw_ref, alog_ref, dtb_ref, nw_ref,
                  o_ref, conv_out_ref, s_out_ref,
                  hist_ref, q_ref, k_ref, v_ref, z_ref, beta_ref, cum_ref, y_ref,
                  u_ref, w_ref, qd_ref, kd_ref, aqk_ref, *, nb, lt, c, gp, gs):
    t = pl.program_id(1)
    rows = nb * lt
    cps = lt // c
    qkv_w = 3 * C_WIDTH

    @pl.when(t == 0)
    def _():
        hist_ref[:, 5:8, :] = conv_ref[...]
        s_out_ref[...] = s0_ref[...]

    pc = _dot(x_ref[...], wc_ref[...])
    hist_ref[:, 8:8 + lt, :] = pc[:, :qkv_w].reshape(nb, lt, qkv_w)
    z_ref[...] = pc[:, qkv_w:qkv_w + C_WIDTH]
    ba = pc[:, qkv_w + C_WIDTH:]
    conv = None
    for j in range(CONV_W):
        term = hist_ref[:, 5 + j:5 + j + lt, :] * cw_ref[j:j + 1, :]
        conv = term if conv is None else conv + term
    tail = hist_ref[:, 5 + lt:8 + lt, :]
    conv_out_ref[...] = tail
    hist_ref[:, 5:8, :] = tail
    act = _silu(conv).reshape(rows, qkv_w)
    for h in range(C_HEADS):
        qs = slice(h * C_HEAD, (h + 1) * C_HEAD)
        ks = slice(C_WIDTH + h * C_HEAD, C_WIDTH + (h + 1) * C_HEAD)
        qh = act[:, qs]
        kh = act[:, ks]
        q_ref[:, qs] = qh * (lax.rsqrt(jnp.sum(qh * qh, axis=-1, keepdims=True) + 1e-12) * C_HEAD ** -0.5)
        k_ref[:, qs] = kh * lax.rsqrt(jnp.sum(kh * kh, axis=-1, keepdims=True) + 1e-12)
    v_ref[...] = act[:, 2 * C_WIDTH:]
    beta_ref[...] = _sigmoid(ba)
    g = -jnp.exp(alog_ref[...]) * _softplus(ba + dtb_ref[...])
    cum_ref[...] = _dot_exact_lhs(_segment_cumsum_mask(rows, c), g)

    strict, incl, eye = _tri_masks(c)
    n_chunks = rows // c
    heads = [slice(h * C_HEAD, (h + 1) * C_HEAD) for h in range(C_HEADS)]

    def state_free(gi, carry):
        pos, qh, kh, vh, bh, col, rowv, g_end = [], [], [], [], [], [], [], []
        for j in range(gp):
            row0 = pl.multiple_of((gi * gp + j) * c, c)
            rs = pl.ds(row0, c)
            q_c, k_c, v_c = q_ref[rs, :], k_ref[rs, :], v_ref[rs, :]
            beta_c = beta_ref[rs, :]
            cum_c = cum_ref[rs, :]
            cum_t = cum_c.T
            cum_end = cum_ref[pl.ds(row0 + c - 1, 1), :]
            for h, sl in enumerate(heads):
                pos.append((rs, h, sl))
                qh.append(q_c[:, sl])
                kh.append(k_c[:, sl])
                vh.append(v_c[:, sl])
                bh.append(beta_c[:, h:h + 1])
                col.append(cum_c[:, C_HEADS + h:C_HEADS + h + 1])
                rowv.append(cum_t[C_HEADS + h:C_HEADS + h + 1, :])
                g_end.append(cum_end[:, C_HEADS + h:C_HEADS + h + 1])
        decay = [jnp.where(incl, jnp.exp(jnp.where(incl, c_ - r_, 0.0)), 0.0) for c_, r_ in zip(col, rowv)]
        kb = [k_ * b_ for k_, b_ in zip(kh, bh)]
        gm = [_dot_nt(jnp.concatenate([kb_, q_], axis=0), k_) for kb_, q_, k_ in zip(kb, qh, kh)]
        tinv = _tri_inv_many([jnp.where(strict, -(g_[:c] * d_), 0.0) for g_, d_ in zip(gm, decay)], eye, c)
        ecol = [jnp.exp(c_) for c_ in col]
        uw = [_dot(t_, jnp.concatenate([v_ * b_, kb_ * e_], axis=1))
              for t_, v_, b_, kb_, e_ in zip(tinv, vh, bh, kb, ecol)]
        for i, (rs, h, sl) in enumerate(pos):
            u_ref[rs, sl] = uw[i][:, :C_HEAD]
            w_ref[rs, sl] = uw[i][:, C_HEAD:]
            qd_ref[rs, sl] = qh[i] * ecol[i]
            kd_ref[rs, sl] = kh[i] * jnp.exp(g_end[i] - col[i])
            aqk_ref[rs, h * c:(h + 1) * c] = gm[i][c:] * decay[i]
        return carry

    lax.fori_loop(0, n_chunks // gp, state_free, 0)

    def state_step(gi, carry):
        pos, wq, u, aqk, kd, decay_end, st = [], [], [], [], [], [], []
        for j in range(gs):
            i = gi * gs + j
            row0 = pl.multiple_of(i * c, c)
            rs = pl.ds(row0, c)
            b = i // cps
            w_c, qd_c, u_c, kd_c, aqk_c = w_ref[rs, :], qd_ref[rs, :], u_ref[rs, :], kd_ref[rs, :], aqk_ref[rs, :]
            e_end = jnp.exp(cum_ref[pl.ds(row0 + c - 1, 1), :])
            for h, sl in enumerate(heads):
                pos.append((rs, b, h, sl))
                wq.append(jnp.concatenate([w_c[:, sl], qd_c[:, sl]], axis=0))
                u.append(u_c[:, sl])
                kd.append(kd_c[:, sl])
                aqk.append(aqk_c[:, h * c:(h + 1) * c])
                decay_end.append(e_end[:, C_HEADS + h:C_HEADS + h + 1])
                st.append(s_out_ref[b, h])
        ps = [_dot(x, s_) for x, s_ in zip(wq, st)]
        v_new = [u_ - p_[:c] for u_, p_ in zip(u, ps)]
        y = [p_[c:] + _dot(a_, v_) for p_, a_, v_ in zip(ps, aqk, v_new)]
        s_new = [s_ * e_ + _dot_tn(k_, v_) for s_, e_, k_, v_ in zip(st, decay_end, kd, v_new)]
        for (rs, b, h, sl), y_, s_ in zip(pos, y, s_new):
            y_ref[rs, sl] = y_
            s_out_ref[b, h] = s_
        return carry

    lax.fori_loop(0, n_chunks // gs, state_step, 0)

    for h in range(C_HEADS):
        sl = slice(h * C_HEAD, (h + 1) * C_HEAD)
        oh = y_ref[:, sl]
        oh = oh * lax.rsqrt(jnp.mean(oh * oh, axis=-1, keepdims=True) + 1e-6) * nw_ref[...]
        o_ref[:, sl] = (oh * _silu(z_ref[:, sl])).astype(o_ref.dtype)


def _delta_parts(conv, s0, wts, layer, state_layer, bsz, seq, nb, lt, c):
    rows = nb * lt
    tpb = seq // lt
    wl = lambda i, t: (layer, 0, 0)
    qkv_w = 3 * C_WIDTH
    return dict(
        args=[conv, s0, wts["w_c"], wts["delta_conv_w"], wts["delta_a_log"], wts["delta_dt_bias"],
              wts["delta_norm_w"]],
        in_specs=[
            pl.BlockSpec((None, nb, CONV_W - 1, qkv_w), lambda i, t: (state_layer, i, 0, 0)),
            pl.BlockSpec((None, nb, C_HEADS, C_HEAD, C_HEAD), lambda i, t: (state_layer, i, 0, 0, 0)),
            pl.BlockSpec((None, D_MODEL, C_COLS_PAD), wl),
            pl.BlockSpec((None, CONV_W, qkv_w), wl),
            pl.BlockSpec((None, 1, 128), wl),
            pl.BlockSpec((None, 1, 128), wl),
            pl.BlockSpec((None, 1, C_HEAD), wl),
        ],
        out_specs=[
            pl.BlockSpec((rows, C_WIDTH), lambda i, t: (i * tpb + t, 0)),
            pl.BlockSpec((nb, CONV_W - 1, qkv_w), lambda i, t: (i, 0, 0)),
            pl.BlockSpec((nb, C_HEADS, C_HEAD, C_HEAD), lambda i, t: (i, 0, 0, 0)),
        ],
        out_shape=[
            jax.ShapeDtypeStruct((bsz * seq, C_WIDTH), BF16),
            jax.ShapeDtypeStruct((bsz, CONV_W - 1, qkv_w), F32),
            jax.ShapeDtypeStruct((bsz, C_HEADS, C_HEAD, C_HEAD), F32),
        ],
        scratch=[pltpu.VMEM((nb, 8 + lt, qkv_w), F32)]
        + [pltpu.VMEM((rows, C_WIDTH), F32)] * 4
        + [pltpu.VMEM((rows, 128), F32)] * 2
        + [pltpu.VMEM((rows, C_WIDTH), F32)] * 5
        + [pltpu.VMEM((rows, C_HEADS * c), F32)],
    )


def _mem_kernel(x_ref, k_ref, v_ref, wm_ref, o_ref, q_ref, *, nb, lt):
    q_ref[...] = _dot(x_ref[...], wm_ref[...]) * (M_HEAD ** -0.5)
    lane = lax.broadcasted_iota(jnp.int32, (1, M_WIDTH), 1)

    def seq_body(b, carry):
        rs = pl.ds(pl.multiple_of(b * lt, lt), lt)
        q = q_ref[rs, :]
        keys_t = k_ref[b].astype(BF16)
        vals_t = v_ref[b].astype(BF16)
        out = jnp.zeros((lt, M_WIDTH), F32)
        for h in range(M_HEADS):
            head = (lane >= h * M_HEAD) & (lane < (h + 1) * M_HEAD)
            s = _dot(jnp.where(head, q, 0.0), keys_t)
            p = jnp.exp(s - jnp.max(s, axis=-1, keepdims=True))
            p = p / jnp.sum(p, axis=-1, keepdims=True)
            out = jnp.where(head, _dot_nt(p, vals_t), out)
        o_ref[rs, :] = out.astype(o_ref.dtype)
        return carry

    lax.fori_loop(0, nb, seq_body, 0)


def _mem_parts(mem_k, mem_v, wts, layer, bsz, seq, nb, lt):
    rows = nb * lt
    tpb = seq // lt
    kv_spec = pl.BlockSpec((None, nb, M_WIDTH, N_MEM), lambda i, t: (layer, i, 0, 0))
    return dict(
        args=[mem_k, mem_v, wts["w_m"]],
        in_specs=[kv_spec, kv_spec, pl.BlockSpec((None, D_MODEL, M_WIDTH), lambda i, t: (layer, 0, 0))],
        out_specs=[pl.BlockSpec((rows, M_WIDTH), lambda i, t: (i * tpb + t, 0))],
        out_shape=[jax.ShapeDtypeStruct((bsz * seq, M_WIDTH), BF16)],
        scratch=[pltpu.VMEM((rows, M_WIDTH), F32)],
    )


def _bcm_kernel(*refs, counts, nb, lt, c, pos0, gp, gs):
    x_ref = refs[0]
    groups = []
    start = 1
    for n in counts:
        groups.append(refs[start:start + n])
        start += n
    pool_in, delta_in, mem_in, pool_out, delta_out, mem_out, pool_scr, delta_scr, mem_scr = groups
    _pool_kernel(x_ref, *pool_in, *pool_out, *pool_scr, nb=nb, lt=lt, pos0=pos0)
    _mem_kernel(x_ref, *mem_in, *mem_out, *mem_scr, nb=nb, lt=lt)
    _delta_kernel(x_ref, *delta_in, *delta_out, *delta_scr, nb=nb, lt=lt, c=c, gp=gp, gs=gs)


def _bcm(x, pool_buf, conv_buf, s_delta, mem_k, mem_v, wts, layer, state_layer, bsz, seq, nb, lt, c, pos0):
    rows = nb * lt
    tpb = seq // lt
    n_chunks = rows // c
    gp = min(n_chunks, ITEMS_PER_STEP // C_HEADS)
    gs = gp if lt == c else 1
    parts = [_pool_parts(pool_buf, wts, layer, state_layer, bsz, seq, nb, lt),
             _delta_parts(conv_buf, s_delta, wts, layer, state_layer, bsz, seq, nb, lt, c),
             _mem_parts(mem_k, mem_v, wts, layer, bsz, seq, nb, lt)]
    gather = lambda key: [item for p in parts for item in p[key]]
    counts = tuple(len(p[key]) for key in ("in_specs", "out_specs", "scratch") for p in parts)
    kern = functools.partial(_bcm_kernel, counts=counts, nb=nb, lt=lt, c=c, pos0=pos0, gp=gp, gs=gs)
    ob, pool_n, oc, conv_n, delta_n, om = pl.pallas_call(
        kern,
        grid=(bsz // nb, tpb),
        in_specs=[pl.BlockSpec((rows, D_MODEL), lambda i, t: (i * tpb + t, 0))] + gather("in_specs"),
        out_specs=gather("out_specs"),
        out_shape=gather("out_shape"),
        scratch_shapes=gather("scratch"),
        compiler_params=pltpu.CompilerParams(
            dimension_semantics=("parallel", "arbitrary"), vmem_limit_bytes=VMEM_LIMIT),
        name="bcm",
    )(x, *gather("args"))
    return ob, pool_n, oc, conv_n, delta_n, om


def _prepare_weights(w_in, rwkv_mu, rwkv_w0, rwkv_w_up, rwkv_a0, rwkv_a_up, rwkv_g_up, rwkv_k_k, rwkv_k_a,
                     rwkv_r_k, rwkv_gn_w, rwkv_gn_b, pool_w, pool_scale, delta_conv_w, delta_a_log,
                     delta_dt_bias, delta_norm_w, w_branch, w_out, ffn1_w_in, ffn1_w_out, ffn2_w_in,
                     ffn2_w_out, ln_g, ln_b):
    depth = w_in.shape[0]
    row = lambda p: p.reshape(depth, 1, -1).astype(F32)
    zeros_lora = jnp.zeros((depth, A_LORA // 2, A_WIDTH), F32)
    lane_row = lambda p: jnp.pad(p.astype(F32), ((0, 0), (C_HEADS, 128 - 2 * C_HEADS))).reshape(depth, 1, 128)
    pool_bd = jnp.zeros((depth, B_WIDTH, B_WIDTH), F32)
    for gidx in range(len(POOL_WINDOWS)):
        sl = slice(gidx * B_GROUP, (gidx + 1) * B_GROUP)
        pool_bd = pool_bd.at[:, sl, sl].set(pool_w[:, gidx].astype(F32))
    head = jnp.arange(A_WIDTH) // A_HEAD
    w_c = jnp.pad(w_in[:, :, OFF_C:OFF_M], ((0, 0), (0, 0), (0, C_COLS_PAD - C_COLS)))
    return {
        "w_a": w_in[:, :, OFF_A:OFF_B].astype(BF16),
        "w_b": w_in[:, :, OFF_B:OFF_C].astype(BF16),
        "w_c": w_c.astype(BF16),
        "w_m": w_in[:, :, OFF_M:OFF_G].astype(BF16),
        "w_g": w_in[:, :, OFF_G:].astype(BF16),
        "rwkv_mu": row(rwkv_mu), "rwkv_w0": row(rwkv_w0), "rwkv_a0": row(rwkv_a0),
        "rwkv_w_up": jnp.concatenate([rwkv_w_up.astype(F32), zeros_lora], axis=1).astype(BF16),
        "rwkv_a_up": jnp.concatenate([zeros_lora, rwkv_a_up.astype(F32)], axis=1).astype(BF16),
        "rwkv_g_up": rwkv_g_up.astype(BF16),
        "rwkv_k_k": row(rwkv_k_k), "rwkv_k_a": row(rwkv_k_a), "rwkv_r_k": row(rwkv_r_k),
        "rwkv_gn_w": row(rwkv_gn_w), "rwkv_gn_b": row(rwkv_gn_b),
        "head_ones": (head[:, None] == head[None, :]).astype(BF16),
        "pool_w": pool_bd.astype(BF16), "pool_scale": row(pool_scale),
        "delta_conv_w": delta_conv_w.astype(F32),
        "delta_a_log": lane_row(delta_a_log), "delta_dt_bias": lane_row(delta_dt_bias),
        "delta_norm_w": row(delta_norm_w),
        "wb_a": w_branch[:, 0:A_WIDTH].astype(BF16),
        "wb_b": w_branch[:, A_WIDTH:A_WIDTH + B_WIDTH].astype(BF16),
        "wb_c": w_branch[:, A_WIDTH + B_WIDTH:A_WIDTH + B_WIDTH + C_WIDTH].astype(BF16),
        "wb_m": w_branch[:, A_WIDTH + B_WIDTH + C_WIDTH:].astype(BF16),
        "w_out": w_out.astype(BF16),
        "ffn1_w_in": ffn1_w_in.astype(BF16), "ffn1_w_out": ffn1_w_out.astype(BF16),
        "ffn2_w_in": ffn2_w_in.astype(BF16), "ffn2_w_out": ffn2_w_out.astype(BF16),
        "ln_g": ln_g.reshape(depth * 3, 1, D_MODEL).astype(F32),
        "ln_b": ln_b.reshape(depth * 3, 1, D_MODEL).astype(F32),
    }


def _trunk(x, mem_k, mem_v, shift, s_rwkv, pool_buf, conv_buf, s_delta, wts, bsz, seq, pos0, stacked,
           tm, nb, nb_bcm, lt, c):
    new = []
    for l in range(DEPTH):
        sl = l if stacked else 0
        x = _ffn(x, wts["ffn1_w_in"], wts["ffn1_w_out"], wts["ln_g"], wts["ln_b"], l, 0, tm)
        oa, shift_n, rwkv_n = _rwkv(x, shift, s_rwkv, wts, l, sl, bsz, seq, nb, lt, c)
        ob, pool_n, oc, conv_n, delta_n, om = _bcm(
            x, pool_buf, conv_buf, s_delta, mem_k, mem_v, wts, l, sl, bsz, seq, nb_bcm, lt, c, pos0)
        x = _merge(x, oa, ob, oc, om, wts, l, tm)
        x = _ffn(x, wts["ffn2_w_in"], wts["ffn2_w_out"], wts["ln_g"], wts["ln_b"], l, 2, tm)
        new.append((rwkv_n, shift_n.reshape(bsz, A_COLS), pool_n, delta_n, conv_n))
    return (x,) + tuple(jnp.stack([st[i] for st in new]) for i in range(5))


def kernel(x_prompt, x_sample, mem_prompt, cache_mem_k, cache_mem_v, state_rwkv, state_rwkv_shift, state_pool, state_delta, state_delta_conv, w_in, rwkv_mu, rwkv_w0, rwkv_w_up, rwkv_a0, rwkv_a_up, rwkv_g_up, rwkv_k_k, rwkv_k_a, rwkv_r_k, rwkv_gn_w, rwkv_gn_b, pool_w, pool_scale, delta_conv_w, delta_a_log, delta_dt_bias, delta_norm_w, mem_w_kv, w_branch, w_out, ffn1_w_in, ffn1_w_out, ffn2_w_in, ffn2_w_out, ln_g, ln_b):
    wts = _prepare_weights(w_in, rwkv_mu, rwkv_w0, rwkv_w_up, rwkv_a0, rwkv_a_up, rwkv_g_up, rwkv_k_k,
                           rwkv_k_a, rwkv_r_k, rwkv_gn_w, rwkv_gn_b, pool_w, pool_scale, delta_conv_w,
                           delta_a_log, delta_dt_bias, delta_norm_w, w_branch, w_out, ffn1_w_in,
                           ffn1_w_out, ffn2_w_in, ffn2_w_out, ln_g, ln_b)
    bp, sp, _ = x_prompt.shape
    bs, ss, _ = x_sample.shape
    n_mem = mem_prompt.shape[1]

    to_t = lambda kv: jnp.transpose(kv, (0, 1, 3, 4, 2)).reshape(DEPTH, kv.shape[1], M_WIDTH, n_mem)
    from_t = lambda kv: jnp.transpose(kv.reshape(DEPTH, kv.shape[1], M_HEADS, M_HEAD, n_mem), (0, 1, 4, 2, 3))
    mem_k, mem_v = _memkv(mem_prompt.reshape(bp * n_mem, D_MODEL),
                          jnp.transpose(mem_w_kv, (0, 2, 1)).astype(BF16), bp, n_mem)

    zeros = lambda *shape: jnp.zeros((1, bp) + shape, F32)
    lt_p = min(sp, 256)
    c_p = min(lt_p, 64)
    prompt = _trunk(
        x_prompt.reshape(bp * sp, D_MODEL), mem_k, mem_v,
        zeros(1, A_COLS), zeros(A_HEADS, A_HEAD, A_HEAD), zeros(POOL_BUF, B_WIDTH),
        zeros(CONV_W - 1, 3 * C_WIDTH), zeros(C_HEADS, C_HEAD, C_HEAD),
        wts, bp, sp, 0, False, tm=min(512, bp * sp), nb=1, nb_bcm=1, lt=lt_p, c=c_p)
    nb_s = min(bs, 32)
    sample = _trunk(
        x_sample.reshape(bs * ss, D_MODEL),
        to_t(cache_mem_k), to_t(cache_mem_v),
        state_rwkv_shift.reshape(DEPTH, bs, 1, A_COLS), state_rwkv, state_pool, state_delta_conv, state_delta,
        wts, bs, ss, PAST_LEN, True, tm=min(512, bs * ss), nb=nb_s, nb_bcm=min(bs, 16), lt=ss, c=ss)

    y_p, p_rwkv, p_shift, p_pool, p_delta, p_conv = prompt
    y_s, s_rwkv, s_shift, s_pool, s_delta, s_conv = sample
    return (y_p.reshape(bp, sp, D_MODEL), y_s.reshape(bs, ss, D_MODEL),
            p_rwkv, p_shift, p_pool, p_delta, p_conv,
            from_t(mem_k), from_t(mem_v),
            s_rwkv, s_shift, s_pool, s_delta, s_conv)
```

```python
import functools
import math

import jax
import jax.numpy as jnp
from jax import lax
from jax.experimental import pallas as pl
from jax.experimental.pallas import tpu as pltpu

F32 = jnp.float32
BF16 = jnp.bfloat16

D_MODEL = 1024
DEPTH = 4
PAST_LEN = 16384
A_WIDTH = 512
A_HEAD = 64
A_HEADS = 8
A_PACK = 4
A_LORA = 128
A_G_RANK = 128
A_COLS = 3 * A_WIDTH + A_LORA + A_G_RANK
GN_EPS = 64e-5
B_WIDTH = 512
B_GROUP = 128
POOL_WINDOWS = (2, 4, 8, 16)
POOL_BUF = 15
C_WIDTH = 512
C_HEAD = 128
C_HEADS = 4
CONV_W = 4
C_COLS = 4 * C_WIDTH + 2 * C_HEADS
C_COLS_PAD = 4 * C_WIDTH + 128
N_MEM = 256
M_HEADS = 4
M_HEAD = 64
M_WIDTH = 256
N_BRANCH = 4
D_FF = 2048
FF_CHUNK = 512
ALPHA = (2.0 * DEPTH) ** 0.25
LN_EPS = 1e-5
EXP_M05 = math.exp(-0.5)
ITEMS_PER_STEP = 16

OFF_A = 0
OFF_B = OFF_A + A_COLS
OFF_C = OFF_B + B_WIDTH
OFF_M = OFF_C + C_COLS
OFF_G = OFF_M + M_WIDTH

VMEM_LIMIT = 56 * 1024 * 1024


def _dot(a, b):
    return jnp.dot(a.astype(BF16), b.astype(BF16), preferred_element_type=F32)


def _dot_nt(a, b):
    return lax.dot_general(a.astype(BF16), b.astype(BF16), (((1,), (1,)), ((), ())),
                           preferred_element_type=F32)


def _dot_tn(a, b):
    return lax.dot_general(a.astype(BF16), b.astype(BF16), (((0,), (0,)), ((), ())),
                           preferred_element_type=F32)


def _bf16_pieces(x, n):
    pieces = []
    rest = x
    for _ in range(n):
        p = rest.astype(BF16)
        pieces.append(p)
        rest = rest - p.astype(F32)
    return pieces


def _dot_exact_lhs(m, x, pieces):
    out = None
    for p in _bf16_pieces(x, pieces):
        d = jnp.dot(m, p, preferred_element_type=F32)
        out = d if out is None else out + d
    return out


def _dot_exact_rhs(x, m, pieces):
    out = None
    for p in _bf16_pieces(x, pieces):
        d = jnp.dot(p, m, preferred_element_type=F32)
        out = d if out is None else out + d
    return out


def _layer_norm(z, g, b):
    mu = jnp.mean(z, axis=-1, keepdims=True)
    zc = z - mu
    var = jnp.mean(zc * zc, axis=-1, keepdims=True)
    return zc * lax.rsqrt(var + LN_EPS) * g + b


def _sigmoid(x):
    return 1.0 / (1.0 + jnp.exp(-x))


def _silu(x):
    return x * _sigmoid(x)


def _softplus(x):
    return jnp.maximum(x, 0.0) + jnp.log(1.0 + jnp.exp(-jnp.abs(x)))


def _tri_masks(c):
    row = lax.broadcasted_iota(jnp.int32, (c, c), 0)
    col = lax.broadcasted_iota(jnp.int32, (c, c), 1)
    return row > col, row >= col, (row == col).astype(F32)


def _segment_cumsum_mask(rows, c):
    shift = c.bit_length() - 1
    row = lax.broadcasted_iota(jnp.int32, (rows, rows), 0)
    col = lax.broadcasted_iota(jnp.int32, (rows, rows), 1)
    same = (row >> shift) == (col >> shift)
    return jnp.where(same & (col <= row), 1.0, 0.0).astype(BF16)


def _tri_inv_many(mats, eye, c, as_rhs=lambda p: p):
    t = [eye + a for a in mats]
    if c <= 2:
        return t
    p = [_dot(a, as_rhs(a)) for a in mats]
    k = 4
    while k < c:
        both = [_dot(jnp.concatenate([p_, t_], axis=0), as_rhs(p_)) for p_, t_ in zip(p, t)]
        t = [t_ + b_[c:] for t_, b_ in zip(t, both)]
        p = [b_[:c] for b_ in both]
        k *= 2
    return [t_ + _dot(t_, as_rhs(p_)) for t_, p_ in zip(t, p)]


def _ffn_kernel(x_ref, win_ref, wout_ref, g_ref, b_ref, o_ref):
    x = x_ref[...]
    xb = x.astype(BF16)
    acc = None
    for j in range(D_FF // FF_CHUNK):
        lo = j * FF_CHUNK
        gate = jnp.dot(xb, win_ref[:, lo:lo + FF_CHUNK], preferred_element_type=F32)
        up = jnp.dot(xb, win_ref[:, D_FF + lo:D_FF + lo + FF_CHUNK], preferred_element_type=F32)
        part = _dot(_silu(gate) * up, wout_ref[lo:lo + FF_CHUNK, :])
        acc = part if acc is None else acc + part
    o_ref[...] = _layer_norm(ALPHA * x + 0.5 * acc, g_ref[...], b_ref[...])


def _ffn(x, w_in, w_out, g, b, layer, idx, tm):
    rows = x.shape[0]
    const = lambda i: (layer, 0, 0)
    return pl.pallas_call(
        _ffn_kernel,
        grid=(rows // tm,),
        in_specs=[
            pl.BlockSpec((tm, D_MODEL), lambda i: (i, 0)),
            pl.BlockSpec((None, D_MODEL, 2 * D_FF), const),
            pl.BlockSpec((None, D_FF, D_MODEL), const),
            pl.BlockSpec((None, 1, D_MODEL), lambda i: (layer * 3 + idx, 0, 0)),
            pl.BlockSpec((None, 1, D_MODEL), lambda i: (layer * 3 + idx, 0, 0)),
        ],
        out_specs=pl.BlockSpec((tm, D_MODEL), lambda i: (i, 0)),
        out_shape=jax.ShapeDtypeStruct((rows, D_MODEL), F32),
        compiler_params=pltpu.CompilerParams(
            dimension_semantics=("parallel",), vmem_limit_bytes=VMEM_LIMIT),
        name="ffn",
    )(x, w_in, w_out, g, b)


def _merge_kernel(x_ref, oa_ref, ob_ref, oc_ref, om_ref, wg_ref, wba_ref, wbb_ref, wbc_ref, wbm_ref,
                  wo_ref, g_ref, b_ref, o_ref):
    x = x_ref[...]
    xb = x.astype(BF16)
    merged = None
    for i, (br_ref, wb_ref) in enumerate(
            ((oa_ref, wba_ref), (ob_ref, wbb_ref), (oc_ref, wbc_ref), (om_ref, wbm_ref))):
        gate = _sigmoid(jnp.dot(xb, wg_ref[:, i * D_MODEL:(i + 1) * D_MODEL], preferred_element_type=F32))
        term = gate * jnp.dot(br_ref[...], wb_ref[...], preferred_element_type=F32)
        merged = term if merged is None else merged + term
    mix = _dot(merged, wo_ref[...])
    o_ref[...] = _layer_norm(ALPHA * x + mix, g_ref[...], b_ref[...])


def _merge(x, oa, ob, oc, om, wts, layer, tm):
    rows = x.shape[0]
    const = lambda i: (layer, 0, 0)
    row_spec = lambda w: pl.BlockSpec((tm, w), lambda i: (i, 0))
    return pl.pallas_call(
        _merge_kernel,
        grid=(rows // tm,),
        in_specs=[
            row_spec(D_MODEL), row_spec(A_WIDTH), row_spec(B_WIDTH), row_spec(C_WIDTH), row_spec(M_WIDTH),
            pl.BlockSpec((None, D_MODEL, N_BRANCH * D_MODEL), const),
            pl.BlockSpec((None, A_WIDTH, D_MODEL), const),
            pl.BlockSpec((None, B_WIDTH, D_MODEL), const),
            pl.BlockSpec((None, C_WIDTH, D_MODEL), const),
            pl.BlockSpec((None, M_WIDTH, D_MODEL), const),
            pl.BlockSpec((None, D_MODEL, D_MODEL), const),
            pl.BlockSpec((None, 1, D_MODEL), lambda i: (layer * 3 + 1, 0, 0)),
            pl.BlockSpec((None, 1, D_MODEL), lambda i: (layer * 3 + 1, 0, 0)),
        ],
        out_specs=row_spec(D_MODEL),
        out_shape=jax.ShapeDtypeStruct((rows, D_MODEL), F32),
        compiler_params=pltpu.CompilerParams(
            dimension_semantics=("parallel",), vmem_limit_bytes=VMEM_LIMIT),
        name="merge",
    )(x, oa, ob, oc, om, wts["w_g"], wts["wb_a"], wts["wb_b"], wts["wb_c"], wts["wb_m"], wts["w_out"],
      wts["ln_g"], wts["ln_b"])


def _memkv_kernel(m_ref, w_ref, k_ref, v_ref):
    kv = _dot_nt(w_ref[...], m_ref[...])
    k_ref[...] = kv[:M_WIDTH]
    v_ref[...] = kv[M_WIDTH:]


def _memkv(mem, w_kv_t, bsz, n_mem):
    out = jax.ShapeDtypeStruct((DEPTH, bsz, M_WIDTH, n_mem), F32)
    return pl.pallas_call(
        _memkv_kernel,
        grid=(DEPTH, bsz),
        in_specs=[
            pl.BlockSpec((n_mem, D_MODEL), lambda l, b: (b, 0)),
            pl.BlockSpec((None, 2 * M_WIDTH, D_MODEL), lambda l, b: (l, 0, 0)),
        ],
        out_specs=[pl.BlockSpec((None, None, M_WIDTH, n_mem), lambda l, b: (l, b, 0, 0))] * 2,
        out_shape=[out, out],
        compiler_params=pltpu.CompilerParams(
            dimension_semantics=("parallel", "parallel"), vmem_limit_bytes=VMEM_LIMIT),
        name="memkv",
    )(mem, w_kv_t)


def _rwkv_kernel(x_ref, shift_ref, s0_ref, wa_ref, mu_ref, w0_ref, wup_ref, a0_ref, aup_ref, gup_ref,
                 kk_ref, ka_ref, rk_ref, gnw_ref, gnb_ref, e_ref,
                 o_ref, shift_out_ref, s_out_ref,
                 hist_ref, rt_ref, kt_ref, at_ref, bt_ref, v_ref, ec_ref, y_ref, ahat_ref, uv_ref, yv_ref, arb_ref,
                 *, nb, lt, c, gp, gs):
    t = pl.program_id(1)
    rows = nb * lt
    cps = lt // c

    @pl.when(t == 0)
    def _():
        hist_ref[:, 7:8, :] = shift_ref[...]
        s_out_ref[...] = s0_ref[...]

    pa = _dot(x_ref[...], wa_ref[...])
    hist_ref[:, 8:8 + lt, :] = pa.reshape(nb, lt, A_COLS)
    prev = hist_ref[:, 7:7 + lt, :].reshape(rows, A_COLS)
    last = hist_ref[:, 7 + lt:8 + lt, :]
    shift_out_ref[...] = last
    hist_ref[:, 7:8, :] = last

    xm = pa + (prev - pa) * mu_ref[...]
    r = xm[:, 0:A_WIDTH]
    k = xm[:, A_WIDTH:2 * A_WIDTH]
    v = xm[:, 2 * A_WIDTH:3 * A_WIDTH]
    xwa = xm[:, 3 * A_WIDTH:3 * A_WIDTH + A_LORA]
    xg = xm[:, 3 * A_WIDTH + A_LORA:]
    z = w0_ref[...] + _dot(jnp.tanh(xwa), wup_ref[...])
    lw = -EXP_M05 * _sigmoid(z)
    a = _sigmoid(a0_ref[...] + _dot(xwa, aup_ref[...]))
    g = _dot(_sigmoid(xg), gup_ref[...])
    e = e_ref[...]
    kkr = k * kk_ref[...]
    kk = kkr * lax.rsqrt(_dot_exact_rhs(kkr * kkr, e, 1) + 1e-12)
    k2 = k * (1.0 + (a - 1.0) * ka_ref[...])
    cum = _dot_exact_lhs(_segment_cumsum_mask(rows, c), lw, 2)
    ecum = jnp.exp(cum)
    encum = jnp.exp(-cum)
    rt_ref[...] = r * ecum
    kt_ref[...] = k2 * encum
    bt_ref[...] = kk * a * encum
    at_ref[...] = -kk * jnp.exp(cum - lw)
    v_ref[...] = v
    ec_ref[...] = ecum

    n_chunks = rows // c
    gw = A_PACK * A_HEAD
    xw = A_PACK * c
    groups = [slice(g * gw, (g + 1) * gw) for g in range(A_HEADS // A_PACK)]

    def lane_block_masks(width, block):
        blk = lax.broadcasted_iota(jnp.int32, (c, width), 1) >> (block.bit_length() - 1)
        return [blk == h for h in range(A_PACK)]

    head_masks = lane_block_masks(gw, A_HEAD)
    mat_masks = lane_block_masks(xw, c)

    def block_diag(x, masks):
        return jnp.concatenate([jnp.where(m, x, 0.0) for m in masks], axis=0)

    row1 = lax.broadcasted_iota(jnp.int32, (c, xw), 0)
    col1 = lax.broadcasted_iota(jnp.int32, (c, xw), 1) & (c - 1)
    strict, incl, eye = row1 > col1, row1 >= col1, (row1 == col1).astype(F32)
    row2 = lax.broadcasted_iota(jnp.int32, (2 * c, xw), 0)
    col2 = lax.broadcasted_iota(jnp.int32, (2 * c, xw), 1) & (c - 1)
    strict_incl = jnp.where(row2 < c, row2, row2 - c + 1) > col2

    def state_free(gi, carry):
        pos, rt, kt, at, bt, vv = [], [], [], [], [], []
        for j in range(gp):
            rs = pl.ds(pl.multiple_of((gi * gp + j) * c, c), c)
            tiles = [ref[rs, :] for ref in (rt_ref, kt_ref, at_ref, bt_ref, v_ref)]
            for g, gl in enumerate(groups):
                pos.append((rs, g, gl))
                for dst, tile in zip((rt, kt, at, bt, vv), tiles):
                    dst.append(tile[:, gl])
        ar = [jnp.concatenate([a_, r_], axis=0) for a_, r_ in zip(at, rt)]
        gb = [_dot_nt(x, block_diag(b_, head_masks)) for x, b_ in zip(ar, bt)]
        gk = [_dot_nt(x, block_diag(k_, head_masks)) for x, k_ in zip(ar, kt)]
        a_ab = [jnp.where(strict, x[:c], 0.0) for x in gb]
        tinv = _tri_inv_many(a_ab, eye, c, lambda p: block_diag(p, mat_masks))
        kv = [_dot(jnp.where(strict_incl, x, 0.0), block_diag(v_, head_masks))
              for x, v_ in zip(gk, vv)]
        ahat = [_dot(t_, block_diag(a_, head_masks)) for t_, a_ in zip(tinv, at)]
        uv = [_dot(t_, block_diag(x[:c], head_masks)) for t_, x in zip(tinv, kv)]
        for (rs, g, gl), ah_, uv_, kv_, gb_ in zip(pos, ahat, uv, kv, gb):
            ahat_ref[rs, gl] = ah_
            uv_ref[rs, gl] = uv_
            yv_ref[rs, gl] = kv_[c:]
            arb_ref[rs, g * xw:(g + 1) * xw] = jnp.where(incl, gb_[c:], 0.0)
        return carry

    lax.fori_loop(0, n_chunks // gp, state_free, 0)

    def state_step(gi, carry):
        pos, ahat, rt, uv, yv, arb, vv, bk, wend, st = [], [], [], [], [], [], [], [], [], []
        for j in range(gs):
            i = gi * gs + j
            row0 = pl.multiple_of(i * c, c)
            rs = pl.ds(row0, c)
            b = i // cps
            tiles = [ref[rs, :] for ref in (ahat_ref, rt_ref, uv_ref, yv_ref, v_ref, bt_ref, kt_ref)]
            arb_c = arb_ref[rs, :]
            w_end = ec_ref[pl.ds(row0 + c - 1, 1), :]
            for h in range(A_HEADS):
                sl = slice(h * A_HEAD, (h + 1) * A_HEAD)
                pos.append((rs, b, h, sl))
                ahat.append(tiles[0][:, sl])
                rt.append(tiles[1][:, sl])
                uv.append(tiles[2][:, sl])
                yv.append(tiles[3][:, sl])
                vv.append(tiles[4][:, sl])
                bk.append(jnp.concatenate([tiles[5][:, sl], tiles[6][:, sl]], axis=0))
                arb.append(arb_c[:, h * c:(h + 1) * c])
                wend.append(w_end[:, sl])
                st.append(s_out_ref[b, h])
        ps = [_dot_nt(jnp.concatenate([a_, r_], axis=0), s_) for a_, r_, s_ in zip(ahat, rt, st)]
        u = [p_[:c] + x for p_, x in zip(ps, uv)]
        y = [p_[c:] + _dot(m_, u_) + x for p_, m_, u_, x in zip(ps, arb, u, yv)]
        s_new = [(s_ + _dot_tn(jnp.concatenate([u_, v_], axis=0), bk_)) * w_
                 for s_, u_, v_, bk_, w_ in zip(st, u, vv, bk, wend)]
        for (rs, b, h, sl), y_, s_ in zip(pos, y, s_new):
            y_ref[rs, sl] = y_
            s_out_ref[b, h] = s_
        return carry

    lax.fori_loop(0, n_chunks // gs, state_step, 0)

    y = y_ref[...]
    inv_n = 1.0 / A_HEAD
    mean = _dot_exact_rhs(y, e, 1) * inv_n
    yc = y - mean
    var = _dot_exact_rhs(yc * yc, e, 1) * inv_n
    yn = yc * lax.rsqrt(var + GN_EPS) * gnw_ref[...] + gnb_ref[...]
    bonus = _dot_exact_rhs(r * k2 * rk_ref[...], e, 2) * v
    o_ref[...] = ((yn + bonus) * g).astype(o_ref.dtype)


def _rwkv(x, shift, s0, wts, layer, state_layer, bsz, seq, nb, lt, c):
    rows = nb * lt
    tpb = seq // lt
    n_chunks = rows // c
    gp = min(n_chunks, ITEMS_PER_STEP * A_PACK // (2 * A_HEADS))
    gs = gp if lt == c else 1
    wl = lambda i, t: (layer, 0, 0)
    vec = lambda width: pl.BlockSpec((None, 1, width), wl)
    kern = functools.partial(_rwkv_kernel, nb=nb, lt=lt, c=c, gp=gp, gs=gs)
    return pl.pallas_call(
        kern,
        grid=(bsz // nb, tpb),
        in_specs=[
            pl.BlockSpec((rows, D_MODEL), lambda i, t: (i * tpb + t, 0)),
            pl.BlockSpec((None, nb, 1, A_COLS), lambda i, t: (state_layer, i, 0, 0)),
            pl.BlockSpec((None, nb, A_HEADS, A_HEAD, A_HEAD), lambda i, t: (state_layer, i, 0, 0, 0)),
            pl.BlockSpec((None, D_MODEL, A_COLS), wl),
            vec(A_COLS), vec(A_WIDTH),
            pl.BlockSpec((None, A_LORA, A_WIDTH), wl),
            vec(A_WIDTH),
            pl.BlockSpec((None, A_LORA, A_WIDTH), wl),
            pl.BlockSpec((None, A_G_RANK, A_WIDTH), wl),
            vec(A_WIDTH), vec(A_WIDTH), vec(A_WIDTH), vec(A_WIDTH), vec(A_WIDTH),
            pl.BlockSpec((A_WIDTH, A_WIDTH), lambda i, t: (0, 0)),
        ],
        out_specs=[
            pl.BlockSpec((rows, A_WIDTH), lambda i, t: (i * tpb + t, 0)),
            pl.BlockSpec((nb, 1, A_COLS), lambda i, t: (i, 0, 0)),
            pl.BlockSpec((nb, A_HEADS, A_HEAD, A_HEAD), lambda i, t: (i, 0, 0, 0)),
        ],
        out_shape=[
            jax.ShapeDtypeStruct((bsz * seq, A_WIDTH), BF16),
            jax.ShapeDtypeStruct((bsz, 1, A_COLS), F32),
            jax.ShapeDtypeStruct((bsz, A_HEADS, A_HEAD, A_HEAD), F32),
        ],
        scratch_shapes=[pltpu.VMEM((nb, 8 + lt, A_COLS), F32)]
        + [pltpu.VMEM((rows, A_WIDTH), F32)] * 10
        + [pltpu.VMEM((rows, A_HEADS * c), F32)],
        compiler_params=pltpu.CompilerParams(
            dimension_semantics=("parallel", "arbitrary"), vmem_limit_bytes=VMEM_LIMIT),
        name="rwkv",
    )(x, shift, s0, wts["w_a"], wts["rwkv_mu"], wts["rwkv_w0"], wts["rwkv_w_up"], wts["rwkv_a0"],
      wts["rwkv_a_up"], wts["rwkv_g_up"], wts["rwkv_k_k"], wts["rwkv_k_a"], wts["rwkv_r_k"],
      wts["rwkv_gn_w"], wts["rwkv_gn_b"], wts["head_ones"])


def _pool_kernel(x_ref, buf_ref, wb_ref, pw_ref, ps_ref, o_ref, buf_out_ref, hist_ref, *, nb, lt, pos0):
    t = pl.program_id(1)
    rows = nb * lt

    @pl.when(t == 0)
    def _():
        hist_ref[:, 1:16, :] = buf_ref[...]

    pb = _dot(x_ref[...], wb_ref[...])
    hist_ref[:, 16:16 + lt, :] = pb.reshape(nb, lt, B_WIDTH)
    pos = pos0 + t * lt + lax.broadcasted_iota(jnp.int32, (1, lt, 1), 1)
    parts = []
    for j, w in enumerate(POOL_WINDOWS):
        cs = slice(j * B_GROUP, (j + 1) * B_GROUP)
        cur = hist_ref[:, 16:16 + lt, cs]
        acc = cur
        for d in range(1, w):
            acc = acc + hist_ref[:, 16 - d:16 - d + lt, cs]
        count = jnp.minimum(pos + 1, w).astype(F32)
        parts.append(acc / count - cur)
    pooled = jnp.concatenate(parts, axis=-1).reshape(rows, B_WIDTH)
    o_ref[...] = (_dot(pooled, pw_ref[...]) * ps_ref[...]).astype(o_ref.dtype)
    tail = hist_ref[:, lt:lt + 16, :]
    buf_out_ref[...] = tail[:, 1:16, :]
    hist_ref[:, 0:16, :] = tail


def _pool_parts(buf, wts, layer, state_layer, bsz, seq, nb, lt):
    rows = nb * lt
    tpb = seq // lt
    wl = lambda i, t: (layer, 0, 0)
    return dict(
        args=[buf, wts["w_b"], wts["pool_w"], wts["pool_scale"]],
        in_specs=[
            pl.BlockSpec((None, nb, POOL_BUF, B_WIDTH), lambda i, t: (state_layer, i, 0, 0)),
            pl.BlockSpec((None, D_MODEL, B_WIDTH), wl),
            pl.BlockSpec((None, B_WIDTH, B_WIDTH), wl),
            pl.BlockSpec((None, 1, B_WIDTH), wl),
        ],
        out_specs=[
            pl.BlockSpec((rows, B_WIDTH), lambda i, t: (i * tpb + t, 0)),
            pl.BlockSpec((nb, POOL_BUF, B_WIDTH), lambda i, t: (i, 0, 0)),
        ],
        out_shape=[
            jax.ShapeDtypeStruct((bsz * seq, B_WIDTH), BF16),
            jax.ShapeDtypeStruct((bsz, POOL_BUF, B_WIDTH), F32),
        ],
        scratch=[pltpu.VMEM((nb, 16 + lt, B_WIDTH), F32)],
    )


def _delta_kernel(x_ref, conv_ref, s0_ref, wc_ref, cw_ref, alog_ref, dtb_ref, nw_ref,
                  o_ref, conv_out_ref, s_out_ref,
                  hist_ref, q_ref, k_ref, v_ref, z_ref, beta_ref, cum_ref, y_ref,
                  u_ref, w_ref, qd_ref, kd_ref, aqk_ref, *, nb, lt, c, gp, gs):
    t = pl.program_id(1)
    rows = nb * lt
    cps = lt // c
    qkv_w = 3 * C_WIDTH

    @pl.when(t == 0)
    def _():
        hist_ref[:, 5:8, :] = conv_ref[...]
        s_out_ref[...] = s0_ref[...]

    pc = _dot(x_ref[...], wc_ref[...])
    hist_ref[:, 8:8 + lt, :] = pc[:, :qkv_w].reshape(nb, lt, qkv_w)
    z_ref[...] = pc[:, qkv_w:qkv_w + C_WIDTH]
    ba = pc[:, qkv_w + C_WIDTH:]
    conv = None
    for j in range(CONV_W):
        term = hist_ref[:, 5 + j:5 + j + lt, :] * cw_ref[j:j + 1, :]
        conv = term if conv is None else conv + term
    tail = hist_ref[:, 5 + lt:8 + lt, :]
    conv_out_ref[...] = tail
    hist_ref[:, 5:8, :] = tail
    act = _silu(conv).reshape(rows, qkv_w)
    for h in range(C_HEADS):
        qs = slice(h * C_HEAD, (h + 1) * C_HEAD)
        ks = slice(C_WIDTH + h * C_HEAD, C_WIDTH + (h + 1) * C_HEAD)
        qh = act[:, qs]
        kh = act[:, ks]
        q_ref[:, qs] = qh * (lax.rsqrt(jnp.sum(qh * qh, axis=-1, keepdims=True) + 1e-12) * C_HEAD ** -0.5)
        k_ref[:, qs] = kh * lax.rsqrt(jnp.sum(kh * kh, axis=-1, keepdims=True) + 1e-12)
    v_ref[...] = act[:, 2 * C_WIDTH:]
    beta_ref[...] = _sigmoid(ba)
    g = -jnp.exp(alog_ref[...]) * _softplus(ba + dtb_ref[...])
    cum_ref[...] = _dot_exact_lhs(_segment_cumsum_mask(rows, c), g, 3)

    strict, incl, eye = _tri_masks(c)
    n_chunks = rows // c
    heads = [slice(h * C_HEAD, (h + 1) * C_HEAD) for h in range(C_HEADS)]

    def state_free(gi, carry):
        pos, qh, kh, vh, bh, col, rowv, g_end = [], [], [], [], [], [], [], []
        for j in range(gp):
            row0 = pl.multiple_of((gi * gp + j) * c, c)
            rs = pl.ds(row0, c)
            q_c, k_c, v_c = q_ref[rs, :], k_ref[rs, :], v_ref[rs, :]
            beta_c = beta_ref[rs, :]
            cum_c = cum_ref[rs, :]
            cum_t = cum_c.T
            cum_end = cum_ref[pl.ds(row0 + c - 1, 1), :]
            for h, sl in enumerate(heads):
                pos.append((rs, h, sl))
                qh.append(q_c[:, sl])
                kh.append(k_c[:, sl])
                vh.append(v_c[:, sl])
                bh.append(beta_c[:, h:h + 1])
                col.append(cum_c[:, C_HEADS + h:C_HEADS + h + 1])
                rowv.append(cum_t[C_HEADS + h:C_HEADS + h + 1, :])
                g_end.append(cum_end[:, C_HEADS + h:C_HEADS + h + 1])
        decay = [jnp.where(incl, jnp.exp(jnp.where(incl, c_ - r_, 0.0)), 0.0) for c_, r_ in zip(col, rowv)]
        kb = [k_ * b_ for k_, b_ in zip(kh, bh)]
        gm = [_dot_nt(jnp.concatenate([kb_, q_], axis=0), k_) for kb_, q_, k_ in zip(kb, qh, kh)]
        tinv = _tri_inv_many([jnp.where(strict, -(g_[:c] * d_), 0.0) for g_, d_ in zip(gm, decay)], eye, c)
        ecol = [jnp.exp(c_) for c_ in col]
        uw = [_dot(t_, jnp.concatenate([v_ * b_, kb_ * e_], axis=1))
              for t_, v_, b_, kb_, e_ in zip(tinv, vh, bh, kb, ecol)]
        for i, (rs, h, sl) in enumerate(pos):
            u_ref[rs, sl] = uw[i][:, :C_HEAD]
            w_ref[rs, sl] = uw[i][:, C_HEAD:]
            qd_ref[rs, sl] = qh[i] * ecol[i]
            kd_ref[rs, sl] = kh[i] * jnp.exp(g_end[i] - col[i])
            aqk_ref[rs, h * c:(h + 1) * c] = gm[i][c:] * decay[i]
        return carry

    lax.fori_loop(0, n_chunks // gp, state_free, 0)

    def state_step(gi, carry):
        pos, wq, u, aqk, kd, decay_end, st = [], [], [], [], [], [], []
        for j in range(gs):
            i = gi * gs + j
            row0 = pl.multiple_of(i * c, c)
            rs = pl.ds(row0, c)
            b = i // cps
            w_c, qd_c, u_c, kd_c, aqk_c = w_ref[rs, :], qd_ref[rs, :], u_ref[rs, :], kd_ref[rs, :], aqk_ref[rs, :]
            e_end = jnp.exp(cum_ref[pl.ds(row0 + c - 1, 1), :])
            for h, sl in enumerate(heads):
                pos.append((rs, b, h, sl))
                wq.append(jnp.concatenate([w_c[:, sl], qd_c[:, sl]], axis=0))
                u.append(u_c[:, sl])
                kd.append(kd_c[:, sl])
                aqk.append(aqk_c[:, h * c:(h + 1) * c])
                decay_end.append(e_end[:, C_HEADS + h:C_HEADS + h + 1])
                st.append(s_out_ref[b, h])
        ps = [_dot(x, s_) for x, s_ in zip(wq, st)]
        v_new = [u_ - p_[:c] for u_, p_ in zip(u, ps)]
        y = [p_[c:] + _dot(a_, v_) for p_, a_, v_ in zip(ps, aqk, v_new)]
        s_new = [s_ * e_ + _dot_tn(k_, v_) for s_, e_, k_, v_ in zip(st, decay_end, kd, v_new)]
        for (rs, b, h, sl), y_, s_ in zip(pos, y, s_new):
            y_ref[rs, sl] = y_
            s_out_ref[b, h] = s_
        return carry

    lax.fori_loop(0, n_chunks // gs, state_step, 0)

    for h in range(C_HEADS):
        sl = slice(h * C_HEAD, (h + 1) * C_HEAD)
        oh = y_ref[:, sl]
        oh = oh * lax.rsqrt(jnp.mean(oh * oh, axis=-1, keepdims=True) + 1e-6) * nw_ref[...]
        o_ref[:, sl] = (oh * _silu(z_ref[:, sl])).astype(o_ref.dtype)


def _delta_parts(conv, s0, wts, layer, state_layer, bsz, seq, nb, lt, c):
    rows = nb * lt
    tpb = seq // lt
    wl = lambda i, t: (layer, 0, 0)
    qkv_w = 3 * C_WIDTH
    return dict(
        args=[conv, s0, wts["w_c"], wts["delta_conv_w"], wts["delta_a_log"], wts["delta_dt_bias"],
              wts["delta_norm_w"]],
        in_specs=[
            pl.BlockSpec((None, nb, CONV_W - 1, qkv_w), lambda i, t: (state_layer, i, 0, 0)),
            pl.BlockSpec((None, nb, C_HEADS, C_HEAD, C_HEAD), lambda i, t: (state_layer, i, 0, 0, 0)),
            pl.BlockSpec((None, D_MODEL, C_COLS_PAD), wl),
            pl.BlockSpec((None, CONV_W, qkv_w), wl),
            pl.BlockSpec((None, 1, 128), wl),
            pl.BlockSpec((None, 1, 128), wl),
            pl.BlockSpec((None, 1, C_HEAD), wl),
        ],
        out_specs=[
            pl.BlockSpec((rows, C_WIDTH), lambda i, t: (i * tpb + t, 0)),
            pl.BlockSpec((nb, CONV_W - 1, qkv_w), lambda i, t: (i, 0, 0)),
            pl.BlockSpec((nb, C_HEADS, C_HEAD, C_HEAD), lambda i, t: (i, 0, 0, 0)),
        ],
        out_shape=[
            jax.ShapeDtypeStruct((bsz * seq, C_WIDTH), BF16),
            jax.ShapeDtypeStruct((bsz, CONV_W - 1, qkv_w), F32),
            jax.ShapeDtypeStruct((bsz, C_HEADS, C_HEAD, C_HEAD), F32),
        ],
        scratch=[pltpu.VMEM((nb, 8 + lt, qkv_w), F32)]
        + [pltpu.VMEM((rows, C_WIDTH), F32)] * 4
        + [pltpu.VMEM((rows, 128), F32)] * 2
        + [pltpu.VMEM((rows, C_WIDTH), F32)] * 5
        + [pltpu.VMEM((rows, C_HEADS * c), F32)],
    )


def _mem_kernel(x_ref, k_ref, v_ref, wm_ref, o_ref, q_ref, *, nb, lt):
    q_ref[...] = _dot(x_ref[...], wm_ref[...]) * (M_HEAD ** -0.5)
    lane = lax.broadcasted_iota(jnp.int32, (1, M_WIDTH), 1)

    def seq_body(b, carry):
        rs = pl.ds(pl.multiple_of(b * lt, lt), lt)
        q = q_ref[rs, :]
        keys_t = k_ref[b].astype(BF16)
        vals_t = v_ref[b].astype(BF16)
        heads = [(lane >= h * M_HEAD) & (lane < (h + 1) * M_HEAD) for h in range(M_HEADS)]
        groups = [heads] if lt * M_HEADS <= 128 else [[m] for m in heads]
        out = jnp.zeros((lt, M_WIDTH), F32)
        for group in groups:
            qs = jnp.concatenate([jnp.where(m, q, 0.0) for m in group], axis=0)
            s = _dot(qs, keys_t)
            p = jnp.exp(s - jnp.max(s, axis=-1, keepdims=True))
            p = p / jnp.sum(p, axis=-1, keepdims=True)
            o = _dot_nt(p, vals_t)
            for j, m in enumerate(group):
                out = jnp.where(m, o[j * lt:(j + 1) * lt], out)
        o_ref[rs, :] = out.astype(o_ref.dtype)
        return carry

    lax.fori_loop(0, nb, seq_body, 0)


def _mem_parts(mem_k, mem_v, wts, layer, bsz, seq, nb, lt):
    rows = nb * lt
    tpb = seq // lt
    kv_spec = pl.BlockSpec((None, nb, M_WIDTH, N_MEM), lambda i, t: (layer, i, 0, 0))
    return dict(
        args=[mem_k, mem_v, wts["w_m"]],
        in_specs=[kv_spec, kv_spec, pl.BlockSpec((None, D_MODEL, M_WIDTH), lambda i, t: (layer, 0, 0))],
        out_specs=[pl.BlockSpec((rows, M_WIDTH), lambda i, t: (i * tpb + t, 0))],
        out_shape=[jax.ShapeDtypeStruct((bsz * seq, M_WIDTH), BF16)],
        scratch=[pltpu.VMEM((rows, M_WIDTH), F32)],
    )


def _bcm_kernel(*refs, counts, nb, lt, c, pos0, gp, gs):
    x_ref = refs[0]
    groups = []
    start = 1
    for n in counts:
        groups.append(refs[start:start + n])
        start += n
    pool_in, delta_in, mem_in, pool_out, delta_out, mem_out, pool_scr, delta_scr, mem_scr = groups
    _pool_kernel(x_ref, *pool_in, *pool_out, *pool_scr, nb=nb, lt=lt, pos0=pos0)
    _mem_kernel(x_ref, *mem_in, *mem_out, *mem_scr, nb=nb, lt=lt)
    _delta_kernel(x_ref, *delta_in, *delta_out, *delta_scr, nb=nb, lt=lt, c=c, gp=gp, gs=gs)


def _bcm(x, pool_buf, conv_buf, s_delta, mem_k, mem_v, wts, layer, state_layer, bsz, seq, nb, lt, c, pos0):
    rows = nb * lt
    tpb = seq // lt
    n_chunks = rows // c
    gp = min(n_chunks, ITEMS_PER_STEP // C_HEADS)
    gs = gp if lt == c else 1
    parts = [_pool_parts(pool_buf, wts, layer, state_layer, bsz, seq, nb, lt),
             _delta_parts(conv_buf, s_delta, wts, layer, state_layer, bsz, seq, nb, lt, c),
             _mem_parts(mem_k, mem_v, wts, layer, bsz, seq, nb, lt)]
    gather = lambda key: [item for p in parts for item in p[key]]
    counts = tuple(len(p[key]) for key in ("in_specs", "out_specs", "scratch") for p in parts)
    kern = functools.partial(_bcm_kernel, counts=counts, nb=nb, lt=lt, c=c, pos0=pos0, gp=gp, gs=gs)
    ob, pool_n, oc, conv_n, delta_n, om = pl.pallas_call(
        kern,
        grid=(bsz // nb, tpb),
        in_specs=[pl.BlockSpec((rows, D_MODEL), lambda i, t: (i * tpb + t, 0))] + gather("in_specs"),
        out_specs=gather("out_specs"),
        out_shape=gather("out_shape"),
        scratch_shapes=gather("scratch"),
        compiler_params=pltpu.CompilerParams(
            dimension_semantics=("parallel", "arbitrary"), vmem_limit_bytes=VMEM_LIMIT),
        name="bcm",
    )(x, *gather("args"))
    return ob, pool_n, oc, conv_n, delta_n, om


def _prepare_weights(w_in, rwkv_mu, rwkv_w0, rwkv_w_up, rwkv_a0, rwkv_a_up, rwkv_g_up, rwkv_k_k, rwkv_k_a,
                     rwkv_r_k, rwkv_gn_w, rwkv_gn_b, pool_w, pool_scale, delta_conv_w, delta_a_log,
                     delta_dt_bias, delta_norm_w, w_branch, w_out, ffn1_w_in, ffn1_w_out, ffn2_w_in,
                     ffn2_w_out, ln_g, ln_b):
    depth = w_in.shape[0]
    row = lambda p: p.reshape(depth, 1, -1).astype(F32)
    zeros_lora = jnp.zeros((depth, A_LORA // 2, A_WIDTH), F32)
    lane_row = lambda p: jnp.pad(p.astype(F32), ((0, 0), (C_HEADS, 128 - 2 * C_HEADS))).reshape(depth, 1, 128)
    pool_bd = jnp.zeros((depth, B_WIDTH, B_WIDTH), F32)
    for gidx in range(len(POOL_WINDOWS)):
        sl = slice(gidx * B_GROUP, (gidx + 1) * B_GROUP)
        pool_bd = pool_bd.at[:, sl, sl].set(pool_w[:, gidx].astype(F32))
    head = jnp.arange(A_WIDTH) // A_HEAD
    w_c = jnp.pad(w_in[:, :, OFF_C:OFF_M], ((0, 0), (0, 0), (0, C_COLS_PAD - C_COLS)))
    return {
        "w_a": w_in[:, :, OFF_A:OFF_B].astype(BF16),
        "w_b": w_in[:, :, OFF_B:OFF_C].astype(BF16),
        "w_c": w_c.astype(BF16),
        "w_m": w_in[:, :, OFF_M:OFF_G].astype(BF16),
        "w_g": w_in[:, :, OFF_G:].astype(BF16),
        "rwkv_mu": row(rwkv_mu), "rwkv_w0": row(rwkv_w0), "rwkv_a0": row(rwkv_a0),
        "rwkv_w_up": jnp.concatenate([rwkv_w_up.astype(F32), zeros_lora], axis=1).astype(BF16),
        "rwkv_a_up": jnp.concatenate([zeros_lora, rwkv_a_up.astype(F32)], axis=1).astype(BF16),
        "rwkv_g_up": rwkv_g_up.astype(BF16),
        "rwkv_k_k": row(rwkv_k_k), "rwkv_k_a": row(rwkv_k_a), "rwkv_r_k": row(rwkv_r_k),
        "rwkv_gn_w": row(rwkv_gn_w), "rwkv_gn_b": row(rwkv_gn_b),
        "head_ones": (head[:, None] == head[None, :]).astype(BF16),
        "pool_w": pool_bd.astype(BF16), "pool_scale": row(pool_scale),
        "delta_conv_w": delta_conv_w.astype(F32),
        "delta_a_log": lane_row(delta_a_log), "delta_dt_bias": lane_row(delta_dt_bias),
        "delta_norm_w": row(delta_norm_w),
        "wb_a": w_branch[:, 0:A_WIDTH].astype(BF16),
        "wb_b": w_branch[:, A_WIDTH:A_WIDTH + B_WIDTH].astype(BF16),
        "wb_c": w_branch[:, A_WIDTH + B_WIDTH:A_WIDTH + B_WIDTH + C_WIDTH].astype(BF16),
        "wb_m": w_branch[:, A_WIDTH + B_WIDTH + C_WIDTH:].astype(BF16),
        "w_out": w_out.astype(BF16),
        "ffn1_w_in": ffn1_w_in.astype(BF16), "ffn1_w_out": ffn1_w_out.astype(BF16),
        "ffn2_w_in": ffn2_w_in.astype(BF16), "ffn2_w_out": ffn2_w_out.astype(BF16),
        "ln_g": ln_g.reshape(depth * 3, 1, D_MODEL).astype(F32),
        "ln_b": ln_b.reshape(depth * 3, 1, D_MODEL).astype(F32),
    }


def _trunk(x, mem_k, mem_v, shift, s_rwkv, pool_buf, conv_buf, s_delta, wts, bsz, seq, pos0, stacked,
           tm, nb, nb_bcm, lt, c):
    new = []
    for l in range(DEPTH):
        sl = l if stacked else 0
        x = _ffn(x, wts["ffn1_w_in"], wts["ffn1_w_out"], wts["ln_g"], wts["ln_b"], l, 0, tm)
        oa, shift_n, rwkv_n = _rwkv(x, shift, s_rwkv, wts, l, sl, bsz, seq, nb, lt, c)
        ob, pool_n, oc, conv_n, delta_n, om = _bcm(
            x, pool_buf, conv_buf, s_delta, mem_k, mem_v, wts, l, sl, bsz, seq, nb_bcm, lt, c, pos0)
        x = _merge(x, oa, ob, oc, om, wts, l, tm)
        x = _ffn(x, wts["ffn2_w_in"], wts["ffn2_w_out"], wts["ln_g"], wts["ln_b"], l, 2, tm)
        new.append((rwkv_n, shift_n.reshape(bsz, A_COLS), pool_n, delta_n, conv_n))
    return (x,) + tuple(jnp.stack([st[i] for st in new]) for i in range(5))


def kernel(x_prompt, x_sample, mem_prompt, cache_mem_k, cache_mem_v, state_rwkv, state_rwkv_shift, state_pool, state_delta, state_delta_conv, w_in, rwkv_mu, rwkv_w0, rwkv_w_up, rwkv_a0, rwkv_a_up, rwkv_g_up, rwkv_k_k, rwkv_k_a, rwkv_r_k, rwkv_gn_w, rwkv_gn_b, pool_w, pool_scale, delta_conv_w, delta_a_log, delta_dt_bias, delta_norm_w, mem_w_kv, w_branch, w_out, ffn1_w_in, ffn1_w_out, ffn2_w_in, ffn2_w_out, ln_g, ln_b):
    wts = _prepare_weights(w_in, rwkv_mu, rwkv_w0, rwkv_w_up, rwkv_a0, rwkv_a_up, rwkv_g_up, rwkv_k_k,
                           rwkv_k_a, rwkv_r_k, rwkv_gn_w, rwkv_gn_b, pool_w, pool_scale, delta_conv_w,
                           delta_a_log, delta_dt_bias, delta_norm_w, w_branch, w_out, ffn1_w_in,
                           ffn1_w_out, ffn2_w_in, ffn2_w_out, ln_g, ln_b)
    bp, sp, _ = x_prompt.shape
    bs, ss, _ = x_sample.shape
    n_mem = mem_prompt.shape[1]

    to_t = lambda kv: jnp.transpose(kv, (0, 1, 3, 4, 2)).reshape(DEPTH, kv.shape[1], M_WIDTH, n_mem)
    from_t = lambda kv: jnp.transpose(kv.reshape(DEPTH, kv.shape[1], M_HEADS, M_HEAD, n_mem), (0, 1, 4, 2, 3))
    mem_k, mem_v = _memkv(mem_prompt.reshape(bp * n_mem, D_MODEL),
                          jnp.transpose(mem_w_kv, (0, 2, 1)).astype(BF16), bp, n_mem)

    zeros = lambda *shape: jnp.zeros((1, bp) + shape, F32)
    lt_p = min(sp, 256)
    c_p = min(lt_p, 64)
    prompt = _trunk(
        x_prompt.reshape(bp * sp, D_MODEL), mem_k, mem_v,
        zeros(1, A_COLS), zeros(A_HEADS, A_HEAD, A_HEAD), zeros(POOL_BUF, B_WIDTH),
        zeros(CONV_W - 1, 3 * C_WIDTH), zeros(C_HEADS, C_HEAD, C_HEAD),
        wts, bp, sp, 0, False, tm=min(512, bp * sp), nb=1, nb_bcm=1, lt=lt_p, c=c_p)
    nb_s = min(bs, 32)
    sample = _trunk(
        x_sample.reshape(bs * ss, D_MODEL),
        to_t(cache_mem_k), to_t(cache_mem_v),
        state_rwkv_shift.reshape(DEPTH, bs, 1, A_COLS), state_rwkv, state_pool, state_delta_conv, state_delta,
        wts, bs, ss, PAST_LEN, True, tm=min(512, bs * ss), nb=nb_s, nb_bcm=min(bs, 16), lt=ss, c=ss)

    y_p, p_rwkv, p_shift, p_pool, p_delta, p_conv = prompt
    y_s, s_rwkv, s_shift, s_pool, s_delta, s_conv = sample
    return (y_p.reshape(bp, sp, D_MODEL), y_s.reshape(bs, ss, D_MODEL),
            p_rwkv, p_shift, p_pool, p_delta, p_conv,
            from_t(mem_k), from_t(mem_v),
            s_rwkv, s_shift, s_pool, s_delta, s_conv)
```

```python
import functools
import math

import jax
import jax.numpy as jnp
from jax import lax
from jax.experimental import pallas as pl
from jax.experimental.pallas import tpu as pltpu

F32 = jnp.float32
BF16 = jnp.bfloat16

D_MODEL = 1024
DEPTH = 4
PAST_LEN = 16384
A_WIDTH = 512
A_HEAD = 64
A_HEADS = 8
A_PACK = 4
A_LORA = 128
A_G_RANK = 128
A_COLS = 3 * A_WIDTH + A_LORA + A_G_RANK
GN_EPS = 64e-5
B_WIDTH = 512
B_GROUP = 128
POOL_WINDOWS = (2, 4, 8, 16)
POOL_BUF = 15
C_WIDTH = 512
C_HEAD = 128
C_HEADS = 4
CONV_W = 4
C_COLS = 4 * C_WIDTH + 2 * C_HEADS
C_COLS_PAD = 4 * C_WIDTH + 128
N_MEM = 256
M_HEADS = 4
M_HEAD = 64
M_WIDTH = 256
N_BRANCH = 4
D_FF = 2048
FF_CHUNK = 512
ALPHA = (2.0 * DEPTH) ** 0.25
LN_EPS = 1e-5
EXP_M05 = math.exp(-0.5)
ITEMS_PER_STEP = 16
SHORT_CHUNK_ITEMS = 2
MEM_SEQS_PER_STEP = 4
ROW_TILE = 1024
ROW_SLAB = 512

OFF_A = 0
OFF_B = OFF_A + A_COLS
OFF_C = OFF_B + B_WIDTH
OFF_M = OFF_C + C_COLS
OFF_G = OFF_M + M_WIDTH

VMEM_LIMIT = 56 * 1024 * 1024


def _dot(a, b):
    return jnp.dot(a.astype(BF16), b.astype(BF16), preferred_element_type=F32)


def _dot_nt(a, b):
    return lax.dot_general(a.astype(BF16), b.astype(BF16), (((1,), (1,)), ((), ())),
                           preferred_element_type=F32)


def _dot_tn(a, b):
    return lax.dot_general(a.astype(BF16), b.astype(BF16), (((0,), (0,)), ((), ())),
                           preferred_element_type=F32)


def _bf16_pieces(x, n):
    pieces = []
    rest = x
    for _ in range(n):
        p = rest.astype(BF16)
        pieces.append(p)
        rest = rest - p.astype(F32)
    return pieces


def _dot_exact_lhs(m, x, pieces):
    out = None
    for p in _bf16_pieces(x, pieces):
        d = jnp.dot(m, p, preferred_element_type=F32)
        out = d if out is None else out + d
    return out


def _dot_exact_rhs(x, m, pieces):
    out = None
    for p in _bf16_pieces(x, pieces):
        d = jnp.dot(p, m, preferred_element_type=F32)
        out = d if out is None else out + d
    return out


def _layer_norm(z, g, b):
    mu = jnp.mean(z, axis=-1, keepdims=True)
    zc = z - mu
    var = jnp.mean(zc * zc, axis=-1, keepdims=True)
    return zc * lax.rsqrt(var + LN_EPS) * g + b


def _sigmoid(x):
    return 1.0 / (1.0 + jnp.exp(-x))


def _silu(x):
    return x * _sigmoid(x)


def _softplus(x):
    return jnp.maximum(x, 0.0) + jnp.log(1.0 + jnp.exp(-jnp.abs(x)))


def _tri_masks(c):
    row = lax.broadcasted_iota(jnp.int32, (c, c), 0)
    col = lax.broadcasted_iota(jnp.int32, (c, c), 1)
    return row > col, row >= col, (row == col).astype(F32)


def _segment_cumsum_mask(rows, c):
    shift = c.bit_length() - 1
    row = lax.broadcasted_iota(jnp.int32, (rows, rows), 0)
    col = lax.broadcasted_iota(jnp.int32, (rows, rows), 1)
    same = (row >> shift) == (col >> shift)
    return jnp.where(same & (col <= row), 1.0, 0.0).astype(BF16)


def _tri_inv_many(mats, eye, c, as_rhs=lambda p: p):
    t = [eye + a for a in mats]
    if c <= 2:
        return t
    p = [_dot(a, as_rhs(a)) for a in mats]
    k = 4
    while k < c:
        both = [_dot(jnp.concatenate([p_, t_], axis=0), as_rhs(p_)) for p_, t_ in zip(p, t)]
        t = [t_ + b_[c:] for t_, b_ in zip(t, both)]
        p = [b_[:c] for b_ in both]
        k *= 2
    return [t_ + _dot(t_, as_rhs(p_)) for t_, p_ in zip(t, p)]


def _row_slabs(ref):
    rows = ref.shape[0]
    sub = min(rows, ROW_SLAB)
    return [slice(i * sub, (i + 1) * sub) for i in range(rows // sub)]


def _ffn_kernel(x_ref, win_ref, wout_ref, g_ref, b_ref, o_ref):
    slabs = _row_slabs(x_ref)
    accs = []
    for rs in slabs:
        xb = x_ref[rs, :].astype(BF16)
        acc = None
        for j in range(D_FF // FF_CHUNK):
            lo = j * FF_CHUNK
            gate = jnp.dot(xb, win_ref[:, lo:lo + FF_CHUNK], preferred_element_type=F32)
            up = jnp.dot(xb, win_ref[:, D_FF + lo:D_FF + lo + FF_CHUNK], preferred_element_type=F32)
            part = _dot(_silu(gate) * up, wout_ref[lo:lo + FF_CHUNK, :])
            acc = part if acc is None else acc + part
        accs.append(acc)
    for rs, acc in zip(slabs, accs):
        o_ref[rs, :] = _layer_norm(ALPHA * x_ref[rs, :] + 0.5 * acc, g_ref[...], b_ref[...])


def _ffn(x, w_in, w_out, g, b, layer, idx, tm):
    rows = x.shape[0]
    const = lambda i: (layer, 0, 0)
    return pl.pallas_call(
        _ffn_kernel,
        grid=(rows // tm,),
        in_specs=[
            pl.BlockSpec((tm, D_MODEL), lambda i: (i, 0)),
            pl.BlockSpec((None, D_MODEL, 2 * D_FF), const, pipeline_mode=pl.Buffered(1)),
            pl.BlockSpec((None, D_FF, D_MODEL), const, pipeline_mode=pl.Buffered(1)),
            pl.BlockSpec((None, 1, D_MODEL), lambda i: (layer * 3 + idx, 0, 0)),
            pl.BlockSpec((None, 1, D_MODEL), lambda i: (layer * 3 + idx, 0, 0)),
        ],
        out_specs=pl.BlockSpec((tm, D_MODEL), lambda i: (i, 0)),
        out_shape=jax.ShapeDtypeStruct((rows, D_MODEL), F32),
        compiler_params=pltpu.CompilerParams(
            dimension_semantics=("parallel",), vmem_limit_bytes=VMEM_LIMIT),
        name="ffn",
    )(x, w_in, w_out, g, b)


def _merge_kernel(x_ref, oa_ref, ob_ref, oc_ref, om_ref, wg_ref, wba_ref, wbb_ref, wbc_ref, wbm_ref,
                  wo_ref, g_ref, b_ref, o_ref):
    slabs = _row_slabs(x_ref)
    mixes = []
    for rs in slabs:
        xb = x_ref[rs, :].astype(BF16)
        merged = None
        for i, (br_ref, wb_ref) in enumerate(
                ((oa_ref, wba_ref), (ob_ref, wbb_ref), (oc_ref, wbc_ref), (om_ref, wbm_ref))):
            gate = _sigmoid(jnp.dot(xb, wg_ref[:, i * D_MODEL:(i + 1) * D_MODEL], preferred_element_type=F32))
            term = gate * jnp.dot(br_ref[rs, :], wb_ref[...], preferred_element_type=F32)
            merged = term if merged is None else merged + term
        mixes.append(_dot(merged, wo_ref[...]))
    for rs, mix in zip(slabs, mixes):
        o_ref[rs, :] = _layer_norm(ALPHA * x_ref[rs, :] + mix, g_ref[...], b_ref[...])


def _merge(x, oa, ob, oc, om, wts, layer, tm):
    rows = x.shape[0]
    const = lambda i: (layer, 0, 0)
    row_spec = lambda w: pl.BlockSpec((tm, w), lambda i: (i, 0))
    return pl.pallas_call(
        _merge_kernel,
        grid=(rows // tm,),
        in_specs=[
            row_spec(D_MODEL), row_spec(A_WIDTH), row_spec(B_WIDTH), row_spec(C_WIDTH), row_spec(M_WIDTH),
            pl.BlockSpec((None, D_MODEL, N_BRANCH * D_MODEL), const, pipeline_mode=pl.Buffered(1)),
            pl.BlockSpec((None, A_WIDTH, D_MODEL), const, pipeline_mode=pl.Buffered(1)),
            pl.BlockSpec((None, B_WIDTH, D_MODEL), const, pipeline_mode=pl.Buffered(1)),
            pl.BlockSpec((None, C_WIDTH, D_MODEL), const, pipeline_mode=pl.Buffered(1)),
            pl.BlockSpec((None, M_WIDTH, D_MODEL), const, pipeline_mode=pl.Buffered(1)),
            pl.BlockSpec((None, D_MODEL, D_MODEL), const, pipeline_mode=pl.Buffered(1)),
            pl.BlockSpec((None, 1, D_MODEL), lambda i: (layer * 3 + 1, 0, 0)),
            pl.BlockSpec((None, 1, D_MODEL), lambda i: (layer * 3 + 1, 0, 0)),
        ],
        out_specs=row_spec(D_MODEL),
        out_shape=jax.ShapeDtypeStruct((rows, D_MODEL), F32),
        compiler_params=pltpu.CompilerParams(
            dimension_semantics=("parallel",), vmem_limit_bytes=VMEM_LIMIT),
        name="merge",
    )(x, oa, ob, oc, om, wts["w_g"], wts["wb_a"], wts["wb_b"], wts["wb_c"], wts["wb_m"], wts["w_out"],
      wts["ln_g"], wts["ln_b"])


def _memkv_kernel(m_ref, w_ref, k_ref, v_ref):
    kv = _dot_nt(w_ref[...], m_ref[...])
    k_ref[...] = kv[:M_WIDTH]
    v_ref[...] = kv[M_WIDTH:]


def _memkv(mem, w_kv_t, bsz, n_mem):
    out = jax.ShapeDtypeStruct((DEPTH, bsz, M_WIDTH, n_mem), F32)
    return pl.pallas_call(
        _memkv_kernel,
        grid=(DEPTH, bsz),
        in_specs=[
            pl.BlockSpec((n_mem, D_MODEL), lambda l, b: (b, 0)),
            pl.BlockSpec((None, 2 * M_WIDTH, D_MODEL), lambda l, b: (l, 0, 0)),
        ],
        out_specs=[pl.BlockSpec((None, None, M_WIDTH, n_mem), lambda l, b: (l, b, 0, 0))] * 2,
        out_shape=[out, out],
        compiler_params=pltpu.CompilerParams(
            dimension_semantics=("parallel", "parallel"), vmem_limit_bytes=VMEM_LIMIT),
        name="memkv",
    )(mem, w_kv_t)


def _rwkv_kernel(x_ref, shift_ref, s0_ref, wa_ref, mu_ref, w0_ref, wup_ref, a0_ref, aup_ref, gup_ref,
                 kk_ref, ka_ref, rk_ref, gnw_ref, gnb_ref, e_ref,
                 o_ref, shift_out_ref, s_out_ref,
                 hist_ref, rt_ref, kt_ref, at_ref, bt_ref, v_ref, ec_ref, y_ref, ahat_ref, uv_ref, yv_ref, arb_ref,
                 *, nb, lt, c, gp, gs):
    t = pl.program_id(1)
    rows = nb * lt
    cps = lt // c

    @pl.when(t == 0)
    def _():
        hist_ref[:, 7:8, :] = shift_ref[...]
        s_out_ref[...] = s0_ref[...]

    pa = _dot(x_ref[...], wa_ref[...])
    hist_ref[:, 8:8 + lt, :] = pa.reshape(nb, lt, A_COLS)
    prev = hist_ref[:, 7:7 + lt, :].reshape(rows, A_COLS)
    last = hist_ref[:, 7 + lt:8 + lt, :]
    shift_out_ref[...] = last
    hist_ref[:, 7:8, :] = last

    xm = pa + (prev - pa) * mu_ref[...]
    r = xm[:, 0:A_WIDTH]
    k = xm[:, A_WIDTH:2 * A_WIDTH]
    v = xm[:, 2 * A_WIDTH:3 * A_WIDTH]
    xwa = xm[:, 3 * A_WIDTH:3 * A_WIDTH + A_LORA]
    xg = xm[:, 3 * A_WIDTH + A_LORA:]
    z = w0_ref[...] + _dot(jnp.tanh(xwa), wup_ref[...])
    lw = -EXP_M05 * _sigmoid(z)
    a = _sigmoid(a0_ref[...] + _dot(xwa, aup_ref[...]))
    g = _dot(_sigmoid(xg), gup_ref[...])
    e = e_ref[...]
    kkr = k * kk_ref[...]
    kk = kkr * lax.rsqrt(_dot_exact_rhs(kkr * kkr, e, 1) + 1e-12)
    k2 = k * (1.0 + (a - 1.0) * ka_ref[...])
    cum = _dot_exact_lhs(_segment_cumsum_mask(rows, c), lw, 2)
    ecum = jnp.exp(cum)
    encum = jnp.exp(-cum)
    rt_ref[...] = r * ecum
    kt_ref[...] = k2 * encum
    bt_ref[...] = kk * a * encum
    at_ref[...] = -kk * jnp.exp(cum - lw)
    v_ref[...] = v
    ec_ref[...] = ecum

    n_chunks = rows // c
    gw = A_PACK * A_HEAD
    xw = A_PACK * c
    groups = [slice(g * gw, (g + 1) * gw) for g in range(A_HEADS // A_PACK)]

    def lane_block_masks(width, block):
        blk = lax.broadcasted_iota(jnp.int32, (c, width), 1) >> (block.bit_length() - 1)
        return [blk == h for h in range(A_PACK)]

    head_masks = lane_block_masks(gw, A_HEAD)
    mat_masks = lane_block_masks(xw, c)

    def block_diag(x, masks):
        return jnp.concatenate([jnp.where(m, x, 0.0) for m in masks], axis=0)

    row1 = lax.broadcasted_iota(jnp.int32, (c, xw), 0)
    col1 = lax.broadcasted_iota(jnp.int32, (c, xw), 1) & (c - 1)
    strict, incl, eye = row1 > col1, row1 >= col1, (row1 == col1).astype(F32)
    row2 = lax.broadcasted_iota(jnp.int32, (2 * c, xw), 0)
    col2 = lax.broadcasted_iota(jnp.int32, (2 * c, xw), 1) & (c - 1)
    strict_incl = jnp.where(row2 < c, row2, row2 - c + 1) > col2

    def state_free(gi, carry):
        pos, rt, kt, at, bt, vv = [], [], [], [], [], []
        for j in range(gp):
            rs = pl.ds(pl.multiple_of((gi * gp + j) * c, c), c)
            tiles = [ref[rs, :] for ref in (rt_ref, kt_ref, at_ref, bt_ref, v_ref)]
            for g, gl in enumerate(groups):
                pos.append((rs, g, gl))
                for dst, tile in zip((rt, kt, at, bt, vv), tiles):
                    dst.append(tile[:, gl])
        ar = [jnp.concatenate([a_, r_], axis=0) for a_, r_ in zip(at, rt)]
        gb = [_dot_nt(x, block_diag(b_, head_masks)) for x, b_ in zip(ar, bt)]
        gk = [_dot_nt(x, block_diag(k_, head_masks)) for x, k_ in zip(ar, kt)]
        a_ab = [jnp.where(strict, x[:c], 0.0) for x in gb]
        tinv = _tri_inv_many(a_ab, eye, c, lambda p: block_diag(p, mat_masks))
        kv = [_dot(jnp.where(strict_incl, x, 0.0), block_diag(v_, head_masks))
              for x, v_ in zip(gk, vv)]
        ahat = [_dot(t_, block_diag(a_, head_masks)) for t_, a_ in zip(tinv, at)]
        uv = [_dot(t_, block_diag(x[:c], head_masks)) for t_, x in zip(tinv, kv)]
        for (rs, g, gl), ah_, uv_, kv_, gb_ in zip(pos, ahat, uv, kv, gb):
            ahat_ref[rs, gl] = ah_
            uv_ref[rs, gl] = uv_
            yv_ref[rs, gl] = kv_[c:]
            arb_ref[rs, g * xw:(g + 1) * xw] = jnp.where(incl, gb_[c:], 0.0)
        return carry

    lax.fori_loop(0, n_chunks // gp, state_free, 0)

    def state_step(gi, carry):
        pos, ahat, rt, uv, yv, arb, vv, bk, wend, st = [], [], [], [], [], [], [], [], [], []
        for j in range(gs):
            i = gi * gs + j
            row0 = pl.multiple_of(i * c, c)
            rs = pl.ds(row0, c)
            b = i // cps
            tiles = [ref[rs, :] for ref in (ahat_ref, rt_ref, uv_ref, yv_ref, v_ref, bt_ref, kt_ref)]
            arb_c = arb_ref[rs, :]
            w_end = ec_ref[pl.ds(row0 + c - 1, 1), :]
            for h in range(A_HEADS):
                sl = slice(h * A_HEAD, (h + 1) * A_HEAD)
                pos.append((rs, b, h, sl))
                ahat.append(tiles[0][:, sl])
                rt.append(tiles[1][:, sl])
                uv.append(tiles[2][:, sl])
                yv.append(tiles[3][:, sl])
                vv.append(tiles[4][:, sl])
                bk.append(jnp.concatenate([tiles[5][:, sl], tiles[6][:, sl]], axis=0))
                arb.append(arb_c[:, h * c:(h + 1) * c])
                wend.append(w_end[:, sl])
                st.append(s_out_ref[b, h])
        ps = [_dot_nt(jnp.concatenate([a_, r_], axis=0), s_) for a_, r_, s_ in zip(ahat, rt, st)]
        u = [p_[:c] + x for p_, x in zip(ps, uv)]
        y = [p_[c:] + _dot(m_, u_) + x for p_, m_, u_, x in zip(ps, arb, u, yv)]
        s_new = [(s_ + _dot_tn(jnp.concatenate([u_, v_], axis=0), bk_)) * w_
                 for s_, u_, v_, bk_, w_ in zip(st, u, vv, bk, wend)]
        for (rs, b, h, sl), y_, s_ in zip(pos, y, s_new):
            y_ref[rs, sl] = y_
            s_out_ref[b, h] = s_
        return carry

    lax.fori_loop(0, n_chunks // gs, state_step, 0)

    y = y_ref[...]
    inv_n = 1.0 / A_HEAD
    mean = _dot_exact_rhs(y, e, 1) * inv_n
    yc = y - mean
    var = _dot_exact_rhs(yc * yc, e, 1) * inv_n
    yn = yc * lax.rsqrt(var + GN_EPS) * gnw_ref[...] + gnb_ref[...]
    bonus = _dot_exact_rhs(r * k2 * rk_ref[...], e, 2) * v
    o_ref[...] = ((yn + bonus) * g).astype(o_ref.dtype)


def _rwkv(x, shift, s0, wts, layer, state_layer, bsz, seq, nb, lt, c):
    rows = nb * lt
    tpb = seq // lt
    n_chunks = rows // c
    gp = min(n_chunks, ITEMS_PER_STEP * A_PACK // (2 * A_HEADS))
    gs = gp if lt == c else 1
    wl = lambda i, t: (layer, 0, 0)
    vec = lambda width: pl.BlockSpec((None, 1, width), wl)
    kern = functools.partial(_rwkv_kernel, nb=nb, lt=lt, c=c, gp=gp, gs=gs)
    return pl.pallas_call(
        kern,
        grid=(bsz // nb, tpb),
        in_specs=[
            pl.BlockSpec((rows, D_MODEL), lambda i, t: (i * tpb + t, 0)),
            pl.BlockSpec((None, nb, 1, A_COLS), lambda i, t: (state_layer, i, 0, 0)),
            pl.BlockSpec((None, nb, A_HEADS, A_HEAD, A_HEAD), lambda i, t: (state_layer, i, 0, 0, 0)),
            pl.BlockSpec((None, D_MODEL, A_COLS), wl),
            vec(A_COLS), vec(A_WIDTH),
            pl.BlockSpec((None, A_LORA, A_WIDTH), wl),
            vec(A_WIDTH),
            pl.BlockSpec((None, A_LORA, A_WIDTH), wl),
            pl.BlockSpec((None, A_G_RANK, A_WIDTH), wl),
            vec(A_WIDTH), vec(A_WIDTH), vec(A_WIDTH), vec(A_WIDTH), vec(A_WIDTH),
            pl.BlockSpec((A_WIDTH, A_WIDTH), lambda i, t: (0, 0)),
        ],
        out_specs=[
            pl.BlockSpec((rows, A_WIDTH), lambda i, t: (i * tpb + t, 0)),
            pl.BlockSpec((nb, 1, A_COLS), lambda i, t: (i, 0, 0)),
            pl.BlockSpec((nb, A_HEADS, A_HEAD, A_HEAD), lambda i, t: (i, 0, 0, 0)),
        ],
        out_shape=[
            jax.ShapeDtypeStruct((bsz * seq, A_WIDTH), BF16),
            jax.ShapeDtypeStruct((bsz, 1, A_COLS), F32),
            jax.ShapeDtypeStruct((bsz, A_HEADS, A_HEAD, A_HEAD), F32),
        ],
        scratch_shapes=[pltpu.VMEM((nb, 8 + lt, A_COLS), F32)]
        + [pltpu.VMEM((rows, A_WIDTH), F32)] * 10
        + [pltpu.VMEM((rows, A_HEADS * c), F32)],
        compiler_params=pltpu.CompilerParams(
            dimension_semantics=("parallel", "arbitrary"), vmem_limit_bytes=VMEM_LIMIT),
        name="rwkv",
    )(x, shift, s0, wts["w_a"], wts["rwkv_mu"], wts["rwkv_w0"], wts["rwkv_w_up"], wts["rwkv_a0"],
      wts["rwkv_a_up"], wts["rwkv_g_up"], wts["rwkv_k_k"], wts["rwkv_k_a"], wts["rwkv_r_k"],
      wts["rwkv_gn_w"], wts["rwkv_gn_b"], wts["head_ones"])


def _pool_kernel(x_ref, buf_ref, wb_ref, pw_ref, ps_ref, o_ref, buf_out_ref, hist_ref, *, nb, lt, pos0):
    t = pl.program_id(1)
    rows = nb * lt

    @pl.when(t == 0)
    def _():
        hist_ref[:, 1:16, :] = buf_ref[...]

    pb = _dot(x_ref[...], wb_ref[...])
    hist_ref[:, 16:16 + lt, :] = pb.reshape(nb, lt, B_WIDTH)
    pos = pos0 + t * lt + lax.broadcasted_iota(jnp.int32, (1, lt, 1), 1)
    parts = []
    for j, w in enumerate(POOL_WINDOWS):
        cs = slice(j * B_GROUP, (j + 1) * B_GROUP)
        cur = hist_ref[:, 16:16 + lt, cs]
        acc = cur
        for d in range(1, w):
            acc = acc + hist_ref[:, 16 - d:16 - d + lt, cs]
        count = jnp.minimum(pos + 1, w).astype(F32)
        parts.append(acc / count - cur)
    pooled = jnp.concatenate(parts, axis=-1).reshape(rows, B_WIDTH)
    o_ref[...] = (_dot(pooled, pw_ref[...]) * ps_ref[...]).astype(o_ref.dtype)
    tail = hist_ref[:, lt:lt + 16, :]
    buf_out_ref[...] = tail[:, 1:16, :]
    hist_ref[:, 0:16, :] = tail


def _pool_parts(buf, wts, layer, state_layer, bsz, seq, nb, lt):
    rows = nb * lt
    tpb = seq // lt
    wl = lambda i, t: (layer, 0, 0)
    return dict(
        args=[buf, wts["w_b"], wts["pool_w"], wts["pool_scale"]],
        in_specs=[
            pl.BlockSpec((None, nb, POOL_BUF, B_WIDTH), lambda i, t: (state_layer, i, 0, 0)),
            pl.BlockSpec((None, D_MODEL, B_WIDTH), wl),
            pl.BlockSpec((None, B_WIDTH, B_WIDTH), wl),
            pl.BlockSpec((None, 1, B_WIDTH), wl),
        ],
        out_specs=[
            pl.BlockSpec((rows, B_WIDTH), lambda i, t: (i * tpb + t, 0)),
            pl.BlockSpec((nb, POOL_BUF, B_WIDTH), lambda i, t: (i, 0, 0)),
        ],
        out_shape=[
            jax.ShapeDtypeStruct((bsz * seq, B_WIDTH), BF16),
            jax.ShapeDtypeStruct((bsz, POOL_BUF, B_WIDTH), F32),
        ],
        scratch=[pltpu.VMEM((nb, 16 + lt, B_WIDTH), F32)],
    )


def _delta_kernel(pc_ref, conv_ref, s0_ref, wc_ref, cw_ref, alog_ref, dtb_ref, nw_ref,
                  o_ref, conv_out_ref, s_out_ref,
                  hist_ref, q_ref, k_ref, v_ref, beta_ref, cum_ref, y_ref,
                  u_ref, w_ref, qd_ref, kd_ref, aqk_ref, *, nb, lt, c, gp, gs):
    t = pl.program_id(1)
    rows = nb * lt
    cps = lt // c
    qkv_w = 3 * C_WIDTH

    @pl.when(t == 0)
    def _():
        hist_ref[:, 5:8, :] = conv_ref[...]
        s_out_ref[...] = s0_ref[...]

    hist_ref[:, 8:8 + lt, :] = pc_ref[:, :qkv_w].reshape(nb, lt, qkv_w)
    ba = pc_ref[:, qkv_w + C_WIDTH:]
    conv = None
    for j in range(CONV_W):
        term = hist_ref[:, 5 + j:5 + j + lt, :] * cw_ref[j:j + 1, :]
        conv = term if conv is None else conv + term
    tail = hist_ref[:, 5 + lt:8 + lt, :]
    conv_out_ref[...] = tail
    hist_ref[:, 5:8, :] = tail
    act = _silu(conv).reshape(rows, qkv_w)
    for h in range(C_HEADS):
        qs = slice(h * C_HEAD, (h + 1) * C_HEAD)
        ks = slice(C_WIDTH + h * C_HEAD, C_WIDTH + (h + 1) * C_HEAD)
        qh = act[:, qs]
        kh = act[:, ks]
        q_ref[:, qs] = qh * (lax.rsqrt(jnp.sum(qh * qh, axis=-1, keepdims=True) + 1e-12) * C_HEAD ** -0.5)
        k_ref[:, qs] = kh * lax.rsqrt(jnp.sum(kh * kh, axis=-1, keepdims=True) + 1e-12)
    v_ref[...] = act[:, 2 * C_WIDTH:]
    beta_ref[...] = _sigmoid(ba)
    g = -jnp.exp(alog_ref[...]) * _softplus(ba + dtb_ref[...])
    cum_ref[...] = _dot_exact_lhs(_segment_cumsum_mask(rows, c), g, 3)

    strict, incl, eye = _tri_masks(c)
    n_chunks = rows // c
    heads = [slice(h * C_HEAD, (h + 1) * C_HEAD) for h in range(C_HEADS)]

    def state_free(gi, carry):
        pos, qh, kh, vh, bh, col, rowv, g_end = [], [], [], [], [], [], [], []
        for j in range(gp):
            row0 = pl.multiple_of((gi * gp + j) * c, c)
            rs = pl.ds(row0, c)
            q_c, k_c, v_c = q_ref[rs, :], k_ref[rs, :], v_ref[rs, :]
            beta_c = beta_ref[rs, :]
            cum_c = cum_ref[rs, :]
            cum_t = cum_c.T
            cum_end = cum_ref[pl.ds(row0 + c - 1, 1), :]
            for h, sl in enumerate(heads):
                pos.append((rs, h, sl))
                qh.append(q_c[:, sl])
                kh.append(k_c[:, sl])
                vh.append(v_c[:, sl])
                bh.append(beta_c[:, h:h + 1])
                col.append(cum_c[:, C_HEADS + h:C_HEADS + h + 1])
                rowv.append(cum_t[C_HEADS + h:C_HEADS + h + 1, :])
                g_end.append(cum_end[:, C_HEADS + h:C_HEADS + h + 1])
        decay = [jnp.where(incl, jnp.exp(jnp.where(incl, c_ - r_, 0.0)), 0.0) for c_, r_ in zip(col, rowv)]
        kb = [k_ * b_ for k_, b_ in zip(kh, bh)]
        gm = [_dot_nt(jnp.concatenate([kb_, q_], axis=0), k_) for kb_, q_, k_ in zip(kb, qh, kh)]
        tinv = _tri_inv_many([jnp.where(strict, -(g_[:c] * d_), 0.0) for g_, d_ in zip(gm, decay)], eye, c)
        ecol = [jnp.exp(c_) for c_ in col]
        uw = [_dot(t_, jnp.concatenate([v_ * b_, kb_ * e_], axis=1))
              for t_, v_, b_, kb_, e_ in zip(tinv, vh, bh, kb, ecol)]
        for i, (rs, h, sl) in enumerate(pos):
            u_ref[rs, sl] = uw[i][:, :C_HEAD]
            w_ref[rs, sl] = uw[i][:, C_HEAD:]
            qd_ref[rs, sl] = qh[i] * ecol[i]
            kd_ref[rs, sl] = kh[i] * jnp.exp(g_end[i] - col[i])
            aqk_ref[rs, h * c:(h + 1) * c] = gm[i][c:] * decay[i]
        return carry

    lax.fori_loop(0, n_chunks // gp, state_free, 0)

    def state_step(gi, carry):
        pos, wq, u, aqk, kd, decay_end, st = [], [], [], [], [], [], []
        for j in range(gs):
            i = gi * gs + j
            row0 = pl.multiple_of(i * c, c)
            rs = pl.ds(row0, c)
            b = i // cps
            w_c, qd_c, u_c, kd_c, aqk_c = w_ref[rs, :], qd_ref[rs, :], u_ref[rs, :], kd_ref[rs, :], aqk_ref[rs, :]
            e_end = jnp.exp(cum_ref[pl.ds(row0 + c - 1, 1), :])
            for h, sl in enumerate(heads):
                pos.append((rs, b, h, sl))
                wq.append(jnp.concatenate([w_c[:, sl], qd_c[:, sl]], axis=0))
                u.append(u_c[:, sl])
                kd.append(kd_c[:, sl])
                aqk.append(aqk_c[:, h * c:(h + 1) * c])
                decay_end.append(e_end[:, C_HEADS + h:C_HEADS + h + 1])
                st.append(s_out_ref[b, h])
        ps = [_dot(x, s_) for x, s_ in zip(wq, st)]
        v_new = [u_ - p_[:c] for u_, p_ in zip(u, ps)]
        y = [p_[c:] + _dot(a_, v_) for p_, a_, v_ in zip(ps, aqk, v_new)]
        s_new = [s_ * e_ + _dot_tn(k_, v_) for s_, e_, k_, v_ in zip(st, decay_end, kd, v_new)]
        for (rs, b, h, sl), y_, s_ in zip(pos, y, s_new):
            y_ref[rs, sl] = y_
            s_out_ref[b, h] = s_
        return carry

    lax.fori_loop(0, n_chunks // gs, state_step, 0)

    for h in range(C_HEADS):
        sl = slice(h * C_HEAD, (h + 1) * C_HEAD)
        oh = y_ref[:, sl]
        oh = oh * lax.rsqrt(jnp.mean(oh * oh, axis=-1, keepdims=True) + 1e-6) * nw_ref[...]
        zg = pc_ref[:, qkv_w + h * C_HEAD:qkv_w + (h + 1) * C_HEAD]
        o_ref[:, sl] = (oh * _silu(zg)).astype(o_ref.dtype)


def _delta_parts(conv, s0, wts, layer, state_layer, bsz, seq, nb, lt, c):
    rows = nb * lt
    tpb = seq // lt
    wl = lambda i, t: (layer, 0, 0)
    qkv_w = 3 * C_WIDTH
    return dict(
        args=[conv, s0, wts["w_c"], wts["delta_conv_w"], wts["delta_a_log"], wts["delta_dt_bias"],
              wts["delta_norm_w"]],
        in_specs=[
            pl.BlockSpec((None, nb, CONV_W - 1, qkv_w), lambda i, t: (state_layer, i, 0, 0)),
            pl.BlockSpec((None, nb, C_HEADS, C_HEAD, C_HEAD), lambda i, t: (state_layer, i, 0, 0, 0)),
            pl.BlockSpec((None, D_MODEL, C_COLS_PAD), wl),
            pl.BlockSpec((None, CONV_W, qkv_w), wl),
            pl.BlockSpec((None, 1, 128), wl),
            pl.BlockSpec((None, 1, 128), wl),
            pl.BlockSpec((None, 1, C_HEAD), wl),
        ],
        out_specs=[
            pl.BlockSpec((rows, C_WIDTH), lambda i, t: (i * tpb + t, 0)),
            pl.BlockSpec((nb, CONV_W - 1, qkv_w), lambda i, t: (i, 0, 0)),
            pl.BlockSpec((nb, C_HEADS, C_HEAD, C_HEAD), lambda i, t: (i, 0, 0, 0)),
        ],
        out_shape=[
            jax.ShapeDtypeStruct((bsz * seq, C_WIDTH), BF16),
            jax.ShapeDtypeStruct((bsz, CONV_W - 1, qkv_w), F32),
            jax.ShapeDtypeStruct((bsz, C_HEADS, C_HEAD, C_HEAD), F32),
        ],
        scratch=[pltpu.VMEM((nb, 8 + lt, qkv_w), F32)]
        + [pltpu.VMEM((rows, C_WIDTH), F32)] * 3
        + [pltpu.VMEM((rows, 128), F32)] * 2
        + [pltpu.VMEM((rows, C_WIDTH), F32)] * 5
        + [pltpu.VMEM((rows, C_HEADS * c), F32)],
    )


def _mem_kernel(x_ref, k_ref, v_ref, wm_ref, o_ref, q_ref, *, nb, lt, after_scores):
    q_ref[...] = _dot(x_ref[...], wm_ref[...]) * (M_HEAD ** -0.5)
    lane = lax.broadcasted_iota(jnp.int32, (1, M_WIDTH), 1)
    heads = [(lane >= h * M_HEAD) & (lane < (h + 1) * M_HEAD) for h in range(M_HEADS)]
    per_step = min(nb, MEM_SEQS_PER_STEP)
    if nb > 1:
        after_scores()

    def seqs_body(i, carry):
        seqs = [i * per_step + j for j in range(per_step)]
        rows = [pl.ds(pl.multiple_of(b * lt, lt), lt) for b in seqs]
        qs = [jnp.concatenate([jnp.where(m, q_ref[rs, :], 0.0) for m in heads], axis=0) for rs in rows]
        s = [_dot(q_, k_ref[b]) for q_, b in zip(qs, seqs)]
        if nb == 1:
            after_scores()
        p = [jnp.exp(s_ - jnp.max(s_, axis=-1, keepdims=True)) for s_ in s]
        p = [p_ / jnp.sum(p_, axis=-1, keepdims=True) for p_ in p]
        o = [_dot_nt(p_, v_ref[b]) for p_, b in zip(p, seqs)]
        for rs, o_ in zip(rows, o):
            out = jnp.zeros((lt, M_WIDTH), F32)
            for j, m in enumerate(heads):
                out = jnp.where(m, o_[j * lt:(j + 1) * lt], out)
            o_ref[rs, :] = out.astype(o_ref.dtype)
        return carry

    if nb == per_step:
        seqs_body(0, 0)
    else:
        lax.fori_loop(0, nb // per_step, seqs_body, 0)


def _mem_parts(mem_k, mem_v, wts, layer, bsz, seq, nb, lt):
    rows = nb * lt
    tpb = seq // lt
    kv_spec = pl.BlockSpec((None, nb, M_WIDTH, N_MEM), lambda i, t: (layer, i, 0, 0))
    return dict(
        args=[mem_k, mem_v, wts["w_m"]],
        in_specs=[kv_spec, kv_spec, pl.BlockSpec((None, D_MODEL, M_WIDTH), lambda i, t: (layer, 0, 0))],
        out_specs=[pl.BlockSpec((rows, M_WIDTH), lambda i, t: (i * tpb + t, 0))],
        out_shape=[jax.ShapeDtypeStruct((bsz * seq, M_WIDTH), BF16)],
        scratch=[pltpu.VMEM((rows, M_WIDTH), F32)],
    )


def _bcm_kernel(*refs, counts, nb, lt, c, pos0, gp, gs):
    x_ref = refs[0]
    groups = []
    start = 1
    for n in counts:
        groups.append(refs[start:start + n])
        start += n
    pool_in, delta_in, mem_in, pool_out, delta_out, mem_out, pool_scr, delta_scr, mem_scr, (pc_ref,) = groups

    def project_delta():
        pc_ref[...] = _dot(x_ref[...], delta_in[2][...])

    _pool_kernel(x_ref, *pool_in, *pool_out, *pool_scr, nb=nb, lt=lt, pos0=pos0)
    _mem_kernel(x_ref, *mem_in, *mem_out, *mem_scr, nb=nb, lt=lt, after_scores=project_delta)
    _delta_kernel(pc_ref, *delta_in, *delta_out, *delta_scr, nb=nb, lt=lt, c=c, gp=gp, gs=gs)


def _bcm(x, pool_buf, conv_buf, s_delta, mem_k, mem_v, wts, layer, state_layer, bsz, seq, nb, lt, c, pos0):
    rows = nb * lt
    tpb = seq // lt
    n_chunks = rows // c
    items = ITEMS_PER_STEP * (SHORT_CHUNK_ITEMS if lt == c else 1)
    gp = min(n_chunks, items // C_HEADS)
    gs = gp if lt == c else 1
    parts = [_pool_parts(pool_buf, wts, layer, state_layer, bsz, seq, nb, lt),
             _delta_parts(conv_buf, s_delta, wts, layer, state_layer, bsz, seq, nb, lt, c),
             _mem_parts(mem_k, mem_v, wts, layer, bsz, seq, nb, lt)]
    gather = lambda key: [item for p in parts for item in p[key]]
    counts = tuple(len(p[key]) for key in ("in_specs", "out_specs", "scratch") for p in parts) + (1,)
    kern = functools.partial(_bcm_kernel, counts=counts, nb=nb, lt=lt, c=c, pos0=pos0, gp=gp, gs=gs)
    ob, pool_n, oc, conv_n, delta_n, om = pl.pallas_call(
        kern,
        grid=(bsz // nb, tpb),
        in_specs=[pl.BlockSpec((rows, D_MODEL), lambda i, t: (i * tpb + t, 0))] + gather("in_specs"),
        out_specs=gather("out_specs"),
        out_shape=gather("out_shape"),
        scratch_shapes=gather("scratch") + [pltpu.VMEM((rows, C_COLS_PAD), F32)],
        compiler_params=pltpu.CompilerParams(
            dimension_semantics=("parallel", "arbitrary"), vmem_limit_bytes=VMEM_LIMIT),
        name="bcm",
    )(x, *gather("args"))
    return ob, pool_n, oc, conv_n, delta_n, om


def _prepare_weights(w_in, rwkv_mu, rwkv_w0, rwkv_w_up, rwkv_a0, rwkv_a_up, rwkv_g_up, rwkv_k_k, rwkv_k_a,
                     rwkv_r_k, rwkv_gn_w, rwkv_gn_b, pool_w, pool_scale, delta_conv_w, delta_a_log,
                     delta_dt_bias, delta_norm_w, w_branch, w_out, ffn1_w_in, ffn1_w_out, ffn2_w_in,
                     ffn2_w_out, ln_g, ln_b):
    depth = w_in.shape[0]
    row = lambda p: p.reshape(depth, 1, -1).astype(F32)
    zeros_lora = jnp.zeros((depth, A_LORA // 2, A_WIDTH), F32)
    lane_row = lambda p: jnp.pad(p.astype(F32), ((0, 0), (C_HEADS, 128 - 2 * C_HEADS))).reshape(depth, 1, 128)
    pool_bd = jnp.zeros((depth, B_WIDTH, B_WIDTH), F32)
    for gidx in range(len(POOL_WINDOWS)):
        sl = slice(gidx * B_GROUP, (gidx + 1) * B_GROUP)
        pool_bd = pool_bd.at[:, sl, sl].set(pool_w[:, gidx].astype(F32))
    head = jnp.arange(A_WIDTH) // A_HEAD
    w_c = jnp.pad(w_in[:, :, OFF_C:OFF_M], ((0, 0), (0, 0), (0, C_COLS_PAD - C_COLS)))
    return {
        "w_a": w_in[:, :, OFF_A:OFF_B].astype(BF16),
        "w_b": w_in[:, :, OFF_B:OFF_C].astype(BF16),
        "w_c": w_c.astype(BF16),
        "w_m": w_in[:, :, OFF_M:OFF_G].astype(BF16),
        "w_g": w_in[:, :, OFF_G:].astype(BF16),
        "rwkv_mu": row(rwkv_mu), "rwkv_w0": row(rwkv_w0), "rwkv_a0": row(rwkv_a0),
        "rwkv_w_up": jnp.concatenate([rwkv_w_up.astype(F32), zeros_lora], axis=1).astype(BF16),
        "rwkv_a_up": jnp.concatenate([zeros_lora, rwkv_a_up.astype(F32)], axis=1).astype(BF16),
        "rwkv_g_up": rwkv_g_up.astype(BF16),
        "rwkv_k_k": row(rwkv_k_k), "rwkv_k_a": row(rwkv_k_a), "rwkv_r_k": row(rwkv_r_k),
        "rwkv_gn_w": row(rwkv_gn_w), "rwkv_gn_b": row(rwkv_gn_b),
        "head_ones": (head[:, None] == head[None, :]).astype(BF16),
        "pool_w": pool_bd.astype(BF16), "pool_scale": row(pool_scale),
        "delta_conv_w": delta_conv_w.astype(F32),
        "delta_a_log": lane_row(delta_a_log), "delta_dt_bias": lane_row(delta_dt_bias),
        "delta_norm_w": row(delta_norm_w),
        "wb_a": w_branch[:, 0:A_WIDTH].astype(BF16),
        "wb_b": w_branch[:, A_WIDTH:A_WIDTH + B_WIDTH].astype(BF16),
        "wb_c": w_branch[:, A_WIDTH + B_WIDTH:A_WIDTH + B_WIDTH + C_WIDTH].astype(BF16),
        "wb_m": w_branch[:, A_WIDTH + B_WIDTH + C_WIDTH:].astype(BF16),
        "w_out": w_out.astype(BF16),
        "ffn1_w_in": ffn1_w_in.astype(BF16), "ffn1_w_out": ffn1_w_out.astype(BF16),
        "ffn2_w_in": ffn2_w_in.astype(BF16), "ffn2_w_out": ffn2_w_out.astype(BF16),
        "ln_g": ln_g.reshape(depth * 3, 1, D_MODEL).astype(F32),
        "ln_b": ln_b.reshape(depth * 3, 1, D_MODEL).astype(F32),
    }


def _trunk(x, mem_k, mem_v, shift, s_rwkv, pool_buf, conv_buf, s_delta, wts, bsz, seq, pos0, stacked,
           tm, nb, nb_bcm, lt, c):
    new = []
    for l in range(DEPTH):
        sl = l if stacked else 0
        x = _ffn(x, wts["ffn1_w_in"], wts["ffn1_w_out"], wts["ln_g"], wts["ln_b"], l, 0, tm)
        oa, shift_n, rwkv_n = _rwkv(x, shift, s_rwkv, wts, l, sl, bsz, seq, nb, lt, c)
        ob, pool_n, oc, conv_n, delta_n, om = _bcm(
            x, pool_buf, conv_buf, s_delta, mem_k, mem_v, wts, l, sl, bsz, seq, nb_bcm, lt, c, pos0)
        x = _merge(x, oa, ob, oc, om, wts, l, tm)
        x = _ffn(x, wts["ffn2_w_in"], wts["ffn2_w_out"], wts["ln_g"], wts["ln_b"], l, 2, tm)
        new.append((rwkv_n, shift_n.reshape(bsz, A_COLS), pool_n, delta_n, conv_n))
    return (x,) + tuple(jnp.stack([st[i] for st in new]) for i in range(5))


def kernel(x_prompt, x_sample, mem_prompt, cache_mem_k, cache_mem_v, state_rwkv, state_rwkv_shift, state_pool, state_delta, state_delta_conv, w_in, rwkv_mu, rwkv_w0, rwkv_w_up, rwkv_a0, rwkv_a_up, rwkv_g_up, rwkv_k_k, rwkv_k_a, rwkv_r_k, rwkv_gn_w, rwkv_gn_b, pool_w, pool_scale, delta_conv_w, delta_a_log, delta_dt_bias, delta_norm_w, mem_w_kv, w_branch, w_out, ffn1_w_in, ffn1_w_out, ffn2_w_in, ffn2_w_out, ln_g, ln_b):
    wts = _prepare_weights(w_in, rwkv_mu, rwkv_w0, rwkv_w_up, rwkv_a0, rwkv_a_up, rwkv_g_up, rwkv_k_k,
                           rwkv_k_a, rwkv_r_k, rwkv_gn_w, rwkv_gn_b, pool_w, pool_scale, delta_conv_w,
                           delta_a_log, delta_dt_bias, delta_norm_w, w_branch, w_out, ffn1_w_in,
                           ffn1_w_out, ffn2_w_in, ffn2_w_out, ln_g, ln_b)
    bp, sp, _ = x_prompt.shape
    bs, ss, _ = x_sample.shape
    n_mem = mem_prompt.shape[1]

    to_t = lambda kv: jnp.transpose(kv, (0, 1, 3, 4, 2)).reshape(DEPTH, kv.shape[1], M_WIDTH, n_mem)
    from_t = lambda kv: jnp.transpose(kv.reshape(DEPTH, kv.shape[1], M_HEADS, M_HEAD, n_mem), (0, 1, 4, 2, 3))
    mem_k, mem_v = _memkv(mem_prompt.reshape(bp * n_mem, D_MODEL),
                          jnp.transpose(mem_w_kv, (0, 2, 1)).astype(BF16), bp, n_mem)

    zeros = lambda *shape: jnp.zeros((1, bp) + shape, F32)
    lt_p = min(sp, 256)
    c_p = min(lt_p, 64)
    prompt = _trunk(
        x_prompt.reshape(bp * sp, D_MODEL), mem_k, mem_v,
        zeros(1, A_COLS), zeros(A_HEADS, A_HEAD, A_HEAD), zeros(POOL_BUF, B_WIDTH),
        zeros(CONV_W - 1, 3 * C_WIDTH), zeros(C_HEADS, C_HEAD, C_HEAD),
        wts, bp, sp, 0, False, tm=min(ROW_TILE, bp * sp), nb=1, nb_bcm=1, lt=lt_p, c=c_p)
    nb_s = min(bs, 32)
    sample = _trunk(
        x_sample.reshape(bs * ss, D_MODEL),
        to_t(cache_mem_k), to_t(cache_mem_v),
        state_rwkv_shift.reshape(DEPTH, bs, 1, A_COLS), state_rwkv, state_pool, state_delta_conv, state_delta,
        wts, bs, ss, PAST_LEN, True, tm=min(ROW_TILE, bs * ss), nb=nb_s, nb_bcm=min(bs, 16), lt=ss, c=ss)

    y_p, p_rwkv, p_shift, p_pool, p_delta, p_conv = prompt
    y_s, s_rwkv, s_shift, s_pool, s_delta, s_conv = sample
    return (y_p.reshape(bp, sp, D_MODEL), y_s.reshape(bs, ss, D_MODEL),
            p_rwkv, p_shift, p_pool, p_delta, p_conv,
            from_t(mem_k), from_t(mem_v),
            s_rwkv, s_shift, s_pool, s_delta, s_conv)
```

```python
import functools
import math

import jax
import jax.numpy as jnp
from jax import lax
from jax.experimental import pallas as pl
from jax.experimental.pallas import tpu as pltpu

F32 = jnp.float32
BF16 = jnp.bfloat16

D_MODEL = 1024
DEPTH = 4
PAST_LEN = 16384
A_WIDTH = 512
A_HEAD = 64
A_HEADS = 8
A_PACK = 4
A_LORA = 128
A_G_RANK = 128
A_COLS = 3 * A_WIDTH + A_LORA + A_G_RANK
GN_EPS = 64e-5
B_WIDTH = 512
B_GROUP = 128
POOL_WINDOWS = (2, 4, 8, 16)
POOL_BUF = 15
C_WIDTH = 512
C_HEAD = 128
C_HEADS = 4
CONV_W = 4
C_COLS = 4 * C_WIDTH + 2 * C_HEADS
C_COLS_PAD = 4 * C_WIDTH + 128
N_MEM = 256
M_HEADS = 4
M_HEAD = 64
M_WIDTH = 256
N_BRANCH = 4
D_FF = 2048
FF_CHUNK = 512
ALPHA = (2.0 * DEPTH) ** 0.25
LN_EPS = 1e-5
EXP_M05 = math.exp(-0.5)
ITEMS_PER_STEP = 16
SHORT_CHUNK_ITEMS = 2
MAX_UNROLLED_STATE_STEPS = 4
MEM_SEQS_PER_STEP = 4
ROW_TILE = 1024
ROW_SLAB = 512

OFF_A = 0
OFF_B = OFF_A + A_COLS
OFF_C = OFF_B + B_WIDTH
OFF_M = OFF_C + C_COLS
OFF_G = OFF_M + M_WIDTH

VMEM_LIMIT = 56 * 1024 * 1024


def _dot(a, b):
    return jnp.dot(a.astype(BF16), b.astype(BF16), preferred_element_type=F32)


def _dot_nt(a, b):
    return lax.dot_general(a.astype(BF16), b.astype(BF16), (((1,), (1,)), ((), ())),
                           preferred_element_type=F32)


def _dot_tn(a, b):
    return lax.dot_general(a.astype(BF16), b.astype(BF16), (((0,), (0,)), ((), ())),
                           preferred_element_type=F32)


def _bf16_pieces(x, n):
    pieces = []
    rest = x
    for _ in range(n):
        p = rest.astype(BF16)
        pieces.append(p)
        rest = rest - p.astype(F32)
    return pieces


def _dot_exact_lhs(m, x, pieces):
    out = None
    for p in _bf16_pieces(x, pieces):
        d = jnp.dot(m, p, preferred_element_type=F32)
        out = d if out is None else out + d
    return out


def _dot_exact_rhs(x, m, pieces):
    out = None
    for p in _bf16_pieces(x, pieces):
        d = jnp.dot(p, m, preferred_element_type=F32)
        out = d if out is None else out + d
    return out


def _layer_norm(z, g, b):
    mu = jnp.mean(z, axis=-1, keepdims=True)
    zc = z - mu
    var = jnp.mean(zc * zc, axis=-1, keepdims=True)
    return zc * lax.rsqrt(var + LN_EPS) * g + b


def _sigmoid(x):
    return 1.0 / (1.0 + jnp.exp(-x))


def _silu(x):
    return x * _sigmoid(x)


def _softplus(x):
    return jnp.maximum(x, 0.0) + jnp.log(1.0 + jnp.exp(-jnp.abs(x)))


def _aligned(index, multiple):
    return index if isinstance(index, int) else pl.multiple_of(index, multiple)


def _tri_masks(c):
    row = lax.broadcasted_iota(jnp.int32, (c, c), 0)
    col = lax.broadcasted_iota(jnp.int32, (c, c), 1)
    return row > col, row >= col, (row == col).astype(F32)


def _segment_cumsum_mask(rows, c):
    shift = c.bit_length() - 1
    row = lax.broadcasted_iota(jnp.int32, (rows, rows), 0)
    col = lax.broadcasted_iota(jnp.int32, (rows, rows), 1)
    same = (row >> shift) == (col >> shift)
    return jnp.where(same & (col <= row), 1.0, 0.0).astype(BF16)


def _tri_inv_many(mats, eye, c, as_rhs=lambda p: p):
    t = [eye + a for a in mats]
    if c <= 2:
        return t
    p = [_dot(a, as_rhs(a)) for a in mats]
    k = 4
    while k < c:
        both = [_dot(jnp.concatenate([p_, t_], axis=0), as_rhs(p_)) for p_, t_ in zip(p, t)]
        t = [t_ + b_[c:] for t_, b_ in zip(t, both)]
        p = [b_[:c] for b_ in both]
        k *= 2
    return [t_ + _dot(t_, as_rhs(p_)) for t_, p_ in zip(t, p)]


def _row_slabs(ref):
    rows = ref.shape[0]
    sub = min(rows, ROW_SLAB)
    return [slice(i * sub, (i + 1) * sub) for i in range(rows // sub)]


def _ffn_kernel(x_ref, win_ref, wout_ref, g_ref, b_ref, o_ref):
    slabs = _row_slabs(x_ref)
    accs = []
    for rs in slabs:
        xb = x_ref[rs, :].astype(BF16)
        acc = None
        for j in range(D_FF // FF_CHUNK):
            lo = j * FF_CHUNK
            gate = jnp.dot(xb, win_ref[:, lo:lo + FF_CHUNK], preferred_element_type=F32)
            up = jnp.dot(xb, win_ref[:, D_FF + lo:D_FF + lo + FF_CHUNK], preferred_element_type=F32)
            part = _dot(_silu(gate) * up, wout_ref[lo:lo + FF_CHUNK, :])
            acc = part if acc is None else acc + part
        accs.append(acc)
    for rs, acc in zip(slabs, accs):
        o_ref[rs, :] = _layer_norm(ALPHA * x_ref[rs, :] + 0.5 * acc, g_ref[...], b_ref[...])


def _ffn(x, w_in, w_out, g, b, layer, idx, tm):
    rows = x.shape[0]
    const = lambda i: (layer, 0, 0)
    return pl.pallas_call(
        _ffn_kernel,
        grid=(rows // tm,),
        in_specs=[
            pl.BlockSpec((tm, D_MODEL), lambda i: (i, 0)),
            pl.BlockSpec((None, D_MODEL, 2 * D_FF), const, pipeline_mode=pl.Buffered(1)),
            pl.BlockSpec((None, D_FF, D_MODEL), const, pipeline_mode=pl.Buffered(1)),
            pl.BlockSpec((None, 1, D_MODEL), lambda i: (layer * 3 + idx, 0, 0)),
            pl.BlockSpec((None, 1, D_MODEL), lambda i: (layer * 3 + idx, 0, 0)),
        ],
        out_specs=pl.BlockSpec((tm, D_MODEL), lambda i: (i, 0)),
        out_shape=jax.ShapeDtypeStruct((rows, D_MODEL), F32),
        compiler_params=pltpu.CompilerParams(
            dimension_semantics=("parallel",), vmem_limit_bytes=VMEM_LIMIT),
        name="ffn",
    )(x, w_in, w_out, g, b)


def _merge_kernel(x_ref, oa_ref, ob_ref, oc_ref, om_ref, wg_ref, wba_ref, wbb_ref, wbc_ref, wbm_ref,
                  wo_ref, g_ref, b_ref, o_ref):
    slabs = _row_slabs(x_ref)
    mixes = []
    for rs in slabs:
        xb = x_ref[rs, :].astype(BF16)
        merged = None
        for i, (br_ref, wb_ref) in enumerate(
                ((oa_ref, wba_ref), (ob_ref, wbb_ref), (oc_ref, wbc_ref), (om_ref, wbm_ref))):
            gate = _sigmoid(jnp.dot(xb, wg_ref[:, i * D_MODEL:(i + 1) * D_MODEL], preferred_element_type=F32))
            term = gate * jnp.dot(br_ref[rs, :], wb_ref[...], preferred_element_type=F32)
            merged = term if merged is None else merged + term
        mixes.append(_dot(merged, wo_ref[...]))
    for rs, mix in zip(slabs, mixes):
        o_ref[rs, :] = _layer_norm(ALPHA * x_ref[rs, :] + mix, g_ref[...], b_ref[...])


def _merge(x, oa, ob, oc, om, wts, layer, tm):
    rows = x.shape[0]
    const = lambda i: (layer, 0, 0)
    row_spec = lambda w: pl.BlockSpec((tm, w), lambda i: (i, 0))
    return pl.pallas_call(
        _merge_kernel,
        grid=(rows // tm,),
        in_specs=[
            row_spec(D_MODEL), row_spec(A_WIDTH), row_spec(B_WIDTH), row_spec(C_WIDTH), row_spec(M_WIDTH),
            pl.BlockSpec((None, D_MODEL, N_BRANCH * D_MODEL), const, pipeline_mode=pl.Buffered(1)),
            pl.BlockSpec((None, A_WIDTH, D_MODEL), const, pipeline_mode=pl.Buffered(1)),
            pl.BlockSpec((None, B_WIDTH, D_MODEL), const, pipeline_mode=pl.Buffered(1)),
            pl.BlockSpec((None, C_WIDTH, D_MODEL), const, pipeline_mode=pl.Buffered(1)),
            pl.BlockSpec((None, M_WIDTH, D_MODEL), const, pipeline_mode=pl.Buffered(1)),
            pl.BlockSpec((None, D_MODEL, D_MODEL), const, pipeline_mode=pl.Buffered(1)),
            pl.BlockSpec((None, 1, D_MODEL), lambda i: (layer * 3 + 1, 0, 0)),
            pl.BlockSpec((None, 1, D_MODEL), lambda i: (layer * 3 + 1, 0, 0)),
        ],
        out_specs=row_spec(D_MODEL),
        out_shape=jax.ShapeDtypeStruct((rows, D_MODEL), F32),
        compiler_params=pltpu.CompilerParams(
            dimension_semantics=("parallel",), vmem_limit_bytes=VMEM_LIMIT),
        name="merge",
    )(x, oa, ob, oc, om, wts["w_g"], wts["wb_a"], wts["wb_b"], wts["wb_c"], wts["wb_m"], wts["w_out"],
      wts["ln_g"], wts["ln_b"])


def _memkv_kernel(m_ref, w_ref, k_ref, v_ref):
    kv = _dot_nt(w_ref[...], m_ref[...])
    k_ref[...] = kv[:M_WIDTH]
    v_ref[...] = kv[M_WIDTH:]


def _memkv(mem, w_kv_t, bsz, n_mem):
    out = jax.ShapeDtypeStruct((DEPTH, bsz, M_WIDTH, n_mem), F32)
    return pl.pallas_call(
        _memkv_kernel,
        grid=(DEPTH, bsz),
        in_specs=[
            pl.BlockSpec((n_mem, D_MODEL), lambda l, b: (b, 0)),
            pl.BlockSpec((None, 2 * M_WIDTH, D_MODEL), lambda l, b: (l, 0, 0)),
        ],
        out_specs=[pl.BlockSpec((None, None, M_WIDTH, n_mem), lambda l, b: (l, b, 0, 0))] * 2,
        out_shape=[out, out],
        compiler_params=pltpu.CompilerParams(
            dimension_semantics=("parallel", "parallel"), vmem_limit_bytes=VMEM_LIMIT),
        name="memkv",
    )(mem, w_kv_t)


def _rwkv_kernel(x_ref, shift_ref, s0_ref, wa_ref, mu_ref, w0_ref, wup_ref, a0_ref, aup_ref, gup_ref,
                 kk_ref, ka_ref, rk_ref, gnw_ref, gnb_ref, e_ref,
                 o_ref, shift_out_ref, s_out_ref,
                 hist_ref, rt_ref, kt_ref, at_ref, bt_ref, v_ref, ec_ref, y_ref, ahat_ref, uv_ref, yv_ref, arb_ref,
                 *, nb, lt, c, gp, gs):
    t = pl.program_id(1)
    rows = nb * lt
    cps = lt // c

    @pl.when(t == 0)
    def _():
        hist_ref[:, 7:8, :] = shift_ref[...]
        s_out_ref[...] = s0_ref[...]

    pa = _dot(x_ref[...], wa_ref[...])
    hist_ref[:, 8:8 + lt, :] = pa.reshape(nb, lt, A_COLS)
    prev = hist_ref[:, 7:7 + lt, :].reshape(rows, A_COLS)
    last = hist_ref[:, 7 + lt:8 + lt, :]
    shift_out_ref[...] = last
    hist_ref[:, 7:8, :] = last

    xm = pa + (prev - pa) * mu_ref[...]
    r = xm[:, 0:A_WIDTH]
    k = xm[:, A_WIDTH:2 * A_WIDTH]
    v = xm[:, 2 * A_WIDTH:3 * A_WIDTH]
    xwa = xm[:, 3 * A_WIDTH:3 * A_WIDTH + A_LORA]
    xg = xm[:, 3 * A_WIDTH + A_LORA:]
    z = w0_ref[...] + _dot(jnp.tanh(xwa), wup_ref[...])
    lw = -EXP_M05 * _sigmoid(z)
    a = _sigmoid(a0_ref[...] + _dot(xwa, aup_ref[...]))
    e = e_ref[...]
    kkr = k * kk_ref[...]
    kk = kkr * lax.rsqrt(_dot_exact_rhs(kkr * kkr, e, 1) + 1e-12)
    k2 = k * (1.0 + (a - 1.0) * ka_ref[...])
    cum = _dot_exact_lhs(_segment_cumsum_mask(rows, c), lw, 2)
    ecum = jnp.exp(cum)
    encum = jnp.exp(-cum)
    rt_ref[...] = r * ecum
    kt_ref[...] = k2 * encum
    bt_ref[...] = kk * a * encum
    at_ref[...] = -kk * jnp.exp(cum - lw)
    v_ref[...] = v
    ec_ref[...] = ecum

    n_chunks = rows // c
    gw = A_PACK * A_HEAD
    xw = A_PACK * c
    groups = [slice(g * gw, (g + 1) * gw) for g in range(A_HEADS // A_PACK)]

    def lane_block_masks(width, block):
        blk = lax.broadcasted_iota(jnp.int32, (c, width), 1) >> (block.bit_length() - 1)
        return [blk == h for h in range(A_PACK)]

    head_masks = lane_block_masks(gw, A_HEAD)
    mat_masks = lane_block_masks(xw, c)

    def block_diag(x, masks):
        return jnp.concatenate([jnp.where(m, x, 0.0) for m in masks], axis=0)

    row1 = lax.broadcasted_iota(jnp.int32, (c, xw), 0)
    col1 = lax.broadcasted_iota(jnp.int32, (c, xw), 1) & (c - 1)
    strict, incl, eye = row1 > col1, row1 >= col1, (row1 == col1).astype(F32)
    row2 = lax.broadcasted_iota(jnp.int32, (2 * c, xw), 0)
    col2 = lax.broadcasted_iota(jnp.int32, (2 * c, xw), 1) & (c - 1)
    strict_incl = jnp.where(row2 < c, row2, row2 - c + 1) > col2

    def state_free(gi, carry):
        pos, rt, kt, at, bt, vv = [], [], [], [], [], []
        for j in range(gp):
            rs = pl.ds(pl.multiple_of((gi * gp + j) * c, c), c)
            tiles = [ref[rs, :] for ref in (rt_ref, kt_ref, at_ref, bt_ref, v_ref)]
            for g, gl in enumerate(groups):
                pos.append((rs, g, gl))
                for dst, tile in zip((rt, kt, at, bt, vv), tiles):
                    dst.append(tile[:, gl])
        ar = [jnp.concatenate([a_, r_], axis=0) for a_, r_ in zip(at, rt)]
        gb = [_dot_nt(x, block_diag(b_, head_masks)) for x, b_ in zip(ar, bt)]
        gk = [_dot_nt(x, block_diag(k_, head_masks)) for x, k_ in zip(ar, kt)]
        a_ab = [jnp.where(strict, x[:c], 0.0) for x in gb]
        tinv = _tri_inv_many(a_ab, eye, c, lambda p: block_diag(p, mat_masks))
        kv = [_dot(jnp.where(strict_incl, x, 0.0), block_diag(v_, head_masks))
              for x, v_ in zip(gk, vv)]
        ahat = [_dot(t_, block_diag(a_, head_masks)) for t_, a_ in zip(tinv, at)]
        uv = [_dot(t_, block_diag(x[:c], head_masks)) for t_, x in zip(tinv, kv)]
        for (rs, g, gl), ah_, uv_, kv_, gb_ in zip(pos, ahat, uv, kv, gb):
            ahat_ref[rs, gl] = ah_
            uv_ref[rs, gl] = uv_
            yv_ref[rs, gl] = kv_[c:]
            arb_ref[rs, g * xw:(g + 1) * xw] = jnp.where(incl, gb_[c:], 0.0)
        return carry

    lax.fori_loop(0, n_chunks // gp, state_free, 0)

    def state_step(gi, carry):
        pos, ahat, rt, uv, yv, arb, vv, bk, wend, st = [], [], [], [], [], [], [], [], [], []
        for j in range(gs):
            i = gi * gs + j
            row0 = _aligned(i * c, c)
            rs = pl.ds(row0, c)
            b = i // cps
            tiles = [ref[rs, :] for ref in (ahat_ref, rt_ref, uv_ref, yv_ref, v_ref, bt_ref, kt_ref)]
            arb_c = arb_ref[rs, :]
            w_end = ec_ref[pl.ds(row0 + c - 1, 1), :]
            for h in range(A_HEADS):
                sl = slice(h * A_HEAD, (h + 1) * A_HEAD)
                pos.append((rs, b, h, sl))
                ahat.append(tiles[0][:, sl])
                rt.append(tiles[1][:, sl])
                uv.append(tiles[2][:, sl])
                yv.append(tiles[3][:, sl])
                vv.append(tiles[4][:, sl])
                bk.append(jnp.concatenate([tiles[5][:, sl], tiles[6][:, sl]], axis=0))
                arb.append(arb_c[:, h * c:(h + 1) * c])
                wend.append(w_end[:, sl])
                st.append(s_out_ref[b, h])
        ps = [_dot_nt(jnp.concatenate([a_, r_], axis=0), s_) for a_, r_, s_ in zip(ahat, rt, st)]
        u = [p_[:c] + x for p_, x in zip(ps, uv)]
        y = [p_[c:] + _dot(m_, u_) + x for p_, m_, u_, x in zip(ps, arb, u, yv)]
        s_new = [(s_ + _dot_tn(jnp.concatenate([u_, v_], axis=0), bk_)) * w_
                 for s_, u_, v_, bk_, w_ in zip(st, u, vv, bk, wend)]
        for (rs, b, h, sl), y_, s_ in zip(pos, y, s_new):
            y_ref[rs, sl] = y_
            s_out_ref[b, h] = s_
        return carry

    late = {}

    def gate():
        late["g"] = _dot(_sigmoid(xg), gup_ref[...])

    def bonus():
        late["bonus"] = _dot_exact_rhs(r * k2 * rk_ref[...], e, 2) * v

    fillers = [gate, bonus]
    n_steps = n_chunks // gs
    if n_steps <= MAX_UNROLLED_STATE_STEPS:
        for i in range(n_steps):
            state_step(i, 0)
            if i < len(fillers):
                fillers[i]()
        for filler in fillers[n_steps:]:
            filler()
    else:
        for filler in fillers:
            filler()
        lax.fori_loop(0, n_steps, state_step, 0)

    y = y_ref[...]
    inv_n = 1.0 / A_HEAD
    mean = _dot_exact_rhs(y, e, 1) * inv_n
    yc = y - mean
    var = _dot_exact_rhs(yc * yc, e, 1) * inv_n
    yn = yc * lax.rsqrt(var + GN_EPS) * gnw_ref[...] + gnb_ref[...]
    o_ref[...] = ((yn + late["bonus"]) * late["g"]).astype(o_ref.dtype)


def _rwkv(x, shift, s0, wts, layer, state_layer, bsz, seq, nb, lt, c):
    rows = nb * lt
    tpb = seq // lt
    n_chunks = rows // c
    gp = min(n_chunks, ITEMS_PER_STEP * A_PACK // (2 * A_HEADS))
    gs = gp if lt == c else 1
    wl = lambda i, t: (layer, 0, 0)
    vec = lambda width: pl.BlockSpec((None, 1, width), wl)
    kern = functools.partial(_rwkv_kernel, nb=nb, lt=lt, c=c, gp=gp, gs=gs)
    return pl.pallas_call(
        kern,
        grid=(bsz // nb, tpb),
        in_specs=[
            pl.BlockSpec((rows, D_MODEL), lambda i, t: (i * tpb + t, 0)),
            pl.BlockSpec((None, nb, 1, A_COLS), lambda i, t: (state_layer, i, 0, 0)),
            pl.BlockSpec((None, nb, A_HEADS, A_HEAD, A_HEAD), lambda i, t: (state_layer, i, 0, 0, 0)),
            pl.BlockSpec((None, D_MODEL, A_COLS), wl),
            vec(A_COLS), vec(A_WIDTH),
            pl.BlockSpec((None, A_LORA, A_WIDTH), wl),
            vec(A_WIDTH),
            pl.BlockSpec((None, A_LORA, A_WIDTH), wl),
            pl.BlockSpec((None, A_G_RANK, A_WIDTH), wl),
            vec(A_WIDTH), vec(A_WIDTH), vec(A_WIDTH), vec(A_WIDTH), vec(A_WIDTH),
            pl.BlockSpec((A_WIDTH, A_WIDTH), lambda i, t: (0, 0)),
        ],
        out_specs=[
            pl.BlockSpec((rows, A_WIDTH), lambda i, t: (i * tpb + t, 0)),
            pl.BlockSpec((nb, 1, A_COLS), lambda i, t: (i, 0, 0)),
            pl.BlockSpec((nb, A_HEADS, A_HEAD, A_HEAD), lambda i, t: (i, 0, 0, 0)),
        ],
        out_shape=[
            jax.ShapeDtypeStruct((bsz * seq, A_WIDTH), BF16),
            jax.ShapeDtypeStruct((bsz, 1, A_COLS), F32),
            jax.ShapeDtypeStruct((bsz, A_HEADS, A_HEAD, A_HEAD), F32),
        ],
        scratch_shapes=[pltpu.VMEM((nb, 8 + lt, A_COLS), F32)]
        + [pltpu.VMEM((rows, A_WIDTH), F32)] * 10
        + [pltpu.VMEM((rows, A_HEADS * c), F32)],
        compiler_params=pltpu.CompilerParams(
            dimension_semantics=("parallel", "arbitrary"), vmem_limit_bytes=VMEM_LIMIT),
        name="rwkv",
    )(x, shift, s0, wts["w_a"], wts["rwkv_mu"], wts["rwkv_w0"], wts["rwkv_w_up"], wts["rwkv_a0"],
      wts["rwkv_a_up"], wts["rwkv_g_up"], wts["rwkv_k_k"], wts["rwkv_k_a"], wts["rwkv_r_k"],
      wts["rwkv_gn_w"], wts["rwkv_gn_b"], wts["head_ones"])


def _pool_init(buf_ref, hist_ref):
    @pl.when(pl.program_id(1) == 0)
    def _():
        hist_ref[:, 1:16, :] = buf_ref[...]


def _pool_kernel(x_ref, buf_ref, wb_ref, pw_ref, ps_ref, o_ref, buf_out_ref, hist_ref, *, nb, lt, pos0):
    t = pl.program_id(1)
    rows = nb * lt
    pb = _dot(x_ref[...], wb_ref[...])
    hist_ref[:, 16:16 + lt, :] = pb.reshape(nb, lt, B_WIDTH)
    pos = pos0 + t * lt + lax.broadcasted_iota(jnp.int32, (1, lt, 1), 1)
    parts = []
    for j, w in enumerate(POOL_WINDOWS):
        cs = slice(j * B_GROUP, (j + 1) * B_GROUP)
        cur = hist_ref[:, 16:16 + lt, cs]
        acc = cur
        for d in range(1, w):
            acc = acc + hist_ref[:, 16 - d:16 - d + lt, cs]
        count = jnp.minimum(pos + 1, w).astype(F32)
        parts.append(acc / count - cur)
    pooled = jnp.concatenate(parts, axis=-1).reshape(rows, B_WIDTH)
    o_ref[...] = (_dot(pooled, pw_ref[...]) * ps_ref[...]).astype(o_ref.dtype)
    tail = hist_ref[:, lt:lt + 16, :]
    buf_out_ref[...] = tail[:, 1:16, :]
    hist_ref[:, 0:16, :] = tail


def _pool_parts(buf, wts, layer, state_layer, bsz, seq, nb, lt):
    rows = nb * lt
    tpb = seq // lt
    wl = lambda i, t: (layer, 0, 0)
    return dict(
        args=[buf, wts["w_b"], wts["pool_w"], wts["pool_scale"]],
        in_specs=[
            pl.BlockSpec((None, nb, POOL_BUF, B_WIDTH), lambda i, t: (state_layer, i, 0, 0)),
            pl.BlockSpec((None, D_MODEL, B_WIDTH), wl),
            pl.BlockSpec((None, B_WIDTH, B_WIDTH), wl),
            pl.BlockSpec((None, 1, B_WIDTH), wl),
        ],
        out_specs=[
            pl.BlockSpec((rows, B_WIDTH), lambda i, t: (i * tpb + t, 0)),
            pl.BlockSpec((nb, POOL_BUF, B_WIDTH), lambda i, t: (i, 0, 0)),
        ],
        out_shape=[
            jax.ShapeDtypeStruct((bsz * seq, B_WIDTH), BF16),
            jax.ShapeDtypeStruct((bsz, POOL_BUF, B_WIDTH), F32),
        ],
        scratch=[pltpu.VMEM((nb, 16 + lt, B_WIDTH), F32)],
    )


def _delta_kernel(pc_ref, conv_ref, s0_ref, wc_ref, cw_ref, alog_ref, dtb_ref, nw_ref,
                  o_ref, conv_out_ref, s_out_ref,
                  hist_ref, q_ref, k_ref, v_ref, beta_ref, cum_ref, y_ref,
                  u_ref, w_ref, qd_ref, kd_ref, aqk_ref, *, nb, lt, c, gp, gs, fillers):
    t = pl.program_id(1)
    rows = nb * lt
    cps = lt // c
    qkv_w = 3 * C_WIDTH

    @pl.when(t == 0)
    def _():
        hist_ref[:, 5:8, :] = conv_ref[...]
        s_out_ref[...] = s0_ref[...]

    hist_ref[:, 8:8 + lt, :] = pc_ref[:, :qkv_w].reshape(nb, lt, qkv_w)
    ba = pc_ref[:, qkv_w + C_WIDTH:]
    conv = None
    for j in range(CONV_W):
        term = hist_ref[:, 5 + j:5 + j + lt, :] * cw_ref[j:j + 1, :]
        conv = term if conv is None else conv + term
    tail = hist_ref[:, 5 + lt:8 + lt, :]
    conv_out_ref[...] = tail
    hist_ref[:, 5:8, :] = tail
    act = _silu(conv).reshape(rows, qkv_w)
    for h in range(C_HEADS):
        qs = slice(h * C_HEAD, (h + 1) * C_HEAD)
        ks = slice(C_WIDTH + h * C_HEAD, C_WIDTH + (h + 1) * C_HEAD)
        qh = act[:, qs]
        kh = act[:, ks]
        q_ref[:, qs] = qh * (lax.rsqrt(jnp.sum(qh * qh, axis=-1, keepdims=True) + 1e-12) * C_HEAD ** -0.5)
        k_ref[:, qs] = kh * lax.rsqrt(jnp.sum(kh * kh, axis=-1, keepdims=True) + 1e-12)
    v_ref[...] = act[:, 2 * C_WIDTH:]
    beta_ref[...] = _sigmoid(ba)
    g = -jnp.exp(alog_ref[...]) * _softplus(ba + dtb_ref[...])
    cum_ref[...] = _dot_exact_lhs(_segment_cumsum_mask(rows, c), g, 3)

    strict, incl, eye = _tri_masks(c)
    n_chunks = rows // c
    heads = [slice(h * C_HEAD, (h + 1) * C_HEAD) for h in range(C_HEADS)]

    def state_free(gi, carry):
        pos, qh, kh, vh, bh, col, rowv, g_end = [], [], [], [], [], [], [], []
        for j in range(gp):
            row0 = pl.multiple_of((gi * gp + j) * c, c)
            rs = pl.ds(row0, c)
            q_c, k_c, v_c = q_ref[rs, :], k_ref[rs, :], v_ref[rs, :]
            beta_c = beta_ref[rs, :]
            cum_c = cum_ref[rs, :]
            cum_t = cum_c.T
            cum_end = cum_ref[pl.ds(row0 + c - 1, 1), :]
            for h, sl in enumerate(heads):
                pos.append((rs, h, sl))
                qh.append(q_c[:, sl])
                kh.append(k_c[:, sl])
                vh.append(v_c[:, sl])
                bh.append(beta_c[:, h:h + 1])
                col.append(cum_c[:, C_HEADS + h:C_HEADS + h + 1])
                rowv.append(cum_t[C_HEADS + h:C_HEADS + h + 1, :])
                g_end.append(cum_end[:, C_HEADS + h:C_HEADS + h + 1])
        decay = [jnp.where(incl, jnp.exp(jnp.where(incl, c_ - r_, 0.0)), 0.0) for c_, r_ in zip(col, rowv)]
        kb = [k_ * b_ for k_, b_ in zip(kh, bh)]
        gm = [_dot_nt(jnp.concatenate([kb_, q_], axis=0), k_) for kb_, q_, k_ in zip(kb, qh, kh)]
        tinv = _tri_inv_many([jnp.where(strict, -(g_[:c] * d_), 0.0) for g_, d_ in zip(gm, decay)], eye, c)
        ecol = [jnp.exp(c_) for c_ in col]
        uw = [_dot(t_, jnp.concatenate([v_ * b_, kb_ * e_], axis=1))
              for t_, v_, b_, kb_, e_ in zip(tinv, vh, bh, kb, ecol)]
        for i, (rs, h, sl) in enumerate(pos):
            u_ref[rs, sl] = uw[i][:, :C_HEAD]
            w_ref[rs, sl] = uw[i][:, C_HEAD:]
            qd_ref[rs, sl] = qh[i] * ecol[i]
            kd_ref[rs, sl] = kh[i] * jnp.exp(g_end[i] - col[i])
            aqk_ref[rs, h * c:(h + 1) * c] = gm[i][c:] * decay[i]
        return carry

    lax.fori_loop(0, n_chunks // gp, state_free, 0)

    def state_step(gi, carry):
        pos, wq, u, aqk, kd, decay_end, st = [], [], [], [], [], [], []
        for j in range(gs):
            i = gi * gs + j
            row0 = _aligned(i * c, c)
            rs = pl.ds(row0, c)
            b = i // cps
            w_c, qd_c, u_c, kd_c, aqk_c = w_ref[rs, :], qd_ref[rs, :], u_ref[rs, :], kd_ref[rs, :], aqk_ref[rs, :]
            e_end = jnp.exp(cum_ref[pl.ds(row0 + c - 1, 1), :])
            for h, sl in enumerate(heads):
                pos.append((rs, b, h, sl))
                wq.append(jnp.concatenate([w_c[:, sl], qd_c[:, sl]], axis=0))
                u.append(u_c[:, sl])
                kd.append(kd_c[:, sl])
                aqk.append(aqk_c[:, h * c:(h + 1) * c])
                decay_end.append(e_end[:, C_HEADS + h:C_HEADS + h + 1])
                st.append(s_out_ref[b, h])
        ps = [_dot(x, s_) for x, s_ in zip(wq, st)]
        v_new = [u_ - p_[:c] for u_, p_ in zip(u, ps)]
        y = [p_[c:] + _dot(a_, v_) for p_, a_, v_ in zip(ps, aqk, v_new)]
        s_new = [s_ * e_ + _dot_tn(k_, v_) for s_, e_, k_, v_ in zip(st, decay_end, kd, v_new)]
        for (rs, b, h, sl), y_, s_ in zip(pos, y, s_new):
            y_ref[rs, sl] = y_
            s_out_ref[b, h] = s_
        return carry

    n_steps = n_chunks // gs
    if n_steps <= MAX_UNROLLED_STATE_STEPS:
        for i in range(n_steps):
            state_step(i, 0)
            if i < len(fillers):
                fillers[i]()
        for filler in fillers[n_steps:]:
            filler()
    else:
        for filler in fillers:
            filler()
        lax.fori_loop(0, n_steps, state_step, 0)

    for h in range(C_HEADS):
        sl = slice(h * C_HEAD, (h + 1) * C_HEAD)
        oh = y_ref[:, sl]
        oh = oh * lax.rsqrt(jnp.mean(oh * oh, axis=-1, keepdims=True) + 1e-6) * nw_ref[...]
        zg = pc_ref[:, qkv_w + h * C_HEAD:qkv_w + (h + 1) * C_HEAD]
        o_ref[:, sl] = (oh * _silu(zg)).astype(o_ref.dtype)


def _delta_parts(conv, s0, wts, layer, state_layer, bsz, seq, nb, lt, c):
    rows = nb * lt
    tpb = seq // lt
    wl = lambda i, t: (layer, 0, 0)
    qkv_w = 3 * C_WIDTH
    return dict(
        args=[conv, s0, wts["w_c"], wts["delta_conv_w"], wts["delta_a_log"], wts["delta_dt_bias"],
              wts["delta_norm_w"]],
        in_specs=[
            pl.BlockSpec((None, nb, CONV_W - 1, qkv_w), lambda i, t: (state_layer, i, 0, 0)),
            pl.BlockSpec((None, nb, C_HEADS, C_HEAD, C_HEAD), lambda i, t: (state_layer, i, 0, 0, 0)),
            pl.BlockSpec((None, D_MODEL, C_COLS_PAD), wl),
            pl.BlockSpec((None, CONV_W, qkv_w), wl),
            pl.BlockSpec((None, 1, 128), wl),
            pl.BlockSpec((None, 1, 128), wl),
            pl.BlockSpec((None, 1, C_HEAD), wl),
        ],
        out_specs=[
            pl.BlockSpec((rows, C_WIDTH), lambda i, t: (i * tpb + t, 0)),
            pl.BlockSpec((nb, CONV_W - 1, qkv_w), lambda i, t: (i, 0, 0)),
            pl.BlockSpec((nb, C_HEADS, C_HEAD, C_HEAD), lambda i, t: (i, 0, 0, 0)),
        ],
        out_shape=[
            jax.ShapeDtypeStruct((bsz * seq, C_WIDTH), BF16),
            jax.ShapeDtypeStruct((bsz, CONV_W - 1, qkv_w), F32),
            jax.ShapeDtypeStruct((bsz, C_HEADS, C_HEAD, C_HEAD), F32),
        ],
        scratch=[pltpu.VMEM((nb, 8 + lt, qkv_w), F32)]
        + [pltpu.VMEM((rows, C_WIDTH), F32)] * 3
        + [pltpu.VMEM((rows, 128), F32)] * 2
        + [pltpu.VMEM((rows, C_WIDTH), F32)] * 5
        + [pltpu.VMEM((rows, C_HEADS * c), F32)],
    )


def _mem_kernel(x_ref, k_ref, v_ref, wm_ref, o_ref, q_ref, *, nb, lt):
    q_ref[...] = _dot(x_ref[...], wm_ref[...]) * (M_HEAD ** -0.5)
    lane = lax.broadcasted_iota(jnp.int32, (1, M_WIDTH), 1)
    heads = [(lane >= h * M_HEAD) & (lane < (h + 1) * M_HEAD) for h in range(M_HEADS)]
    per_step = min(nb, MEM_SEQS_PER_STEP)

    def seqs_body(i, carry):
        seqs = [i * per_step + j for j in range(per_step)]
        rows = [pl.ds(pl.multiple_of(b * lt, lt), lt) for b in seqs]
        qs = [jnp.concatenate([jnp.where(m, q_ref[rs, :], 0.0) for m in heads], axis=0) for rs in rows]
        s = [_dot(q_, k_ref[b]) for q_, b in zip(qs, seqs)]
        p =[jnp.exp(s_ - jnp.max(s_, axis=-1, keepdims=True)) for s_ in s]
        p = [p_ / jnp.sum(p_, axis=-1, keepdims=True) for p_ in p]
        o = [_dot_nt(p_, v_ref[b]) for p_, b in zip(p, seqs)]
        for rs, o_ in zip(rows, o):
            out = jnp.zeros((lt, M_WIDTH), F32)
            for j, m in enumerate(heads):
                out = jnp.where(m, o_[j * lt:(j + 1) * lt], out)
            o_ref[rs, :] = out.astype(o_ref.dtype)
        return carry

    if nb == per_step:
        seqs_body(0, 0)
    else:
        lax.fori_loop(0, nb // per_step, seqs_body, 0)


def _mem_parts(mem_k, mem_v, wts, layer, bsz, seq, nb, lt):
    rows = nb * lt
    tpb = seq // lt
    kv_spec = pl.BlockSpec((None, nb, M_WIDTH, N_MEM), lambda i, t: (layer, i, 0, 0))
    return dict(
        args=[mem_k, mem_v, wts["w_m"]],
        in_specs=[kv_spec, kv_spec, pl.BlockSpec((None, D_MODEL, M_WIDTH), lambda i, t: (layer, 0, 0))],
        out_specs=[pl.BlockSpec((rows, M_WIDTH), lambda i, t: (i * tpb + t, 0))],
        out_shape=[jax.ShapeDtypeStruct((bsz * seq, M_WIDTH), BF16)],
        scratch=[pltpu.VMEM((rows, M_WIDTH), F32)],
    )


def _bcm_kernel(*refs, counts, nb, lt, c, pos0, gp, gs):
    x_ref = refs[0]
    groups = []
    start = 1
    for n in counts:
        groups.append(refs[start:start + n])
        start += n
    pool_in, delta_in, mem_in, pool_out, delta_out, mem_out, pool_scr, delta_scr, mem_scr, (pc_ref,) = groups

    _pool_init(pool_in[0], pool_scr[0])
    pc_ref[...] = _dot(x_ref[...], delta_in[2][...])
    fillers = [
        lambda: _pool_kernel(x_ref, *pool_in, *pool_out, *pool_scr, nb=nb, lt=lt, pos0=pos0),
        lambda: _mem_kernel(x_ref, *mem_in, *mem_out, *mem_scr, nb=nb, lt=lt),
    ]
    _delta_kernel(pc_ref, *delta_in, *delta_out, *delta_scr, nb=nb, lt=lt, c=c, gp=gp, gs=gs,
                  fillers=fillers)


def _bcm(x, pool_buf, conv_buf, s_delta, mem_k, mem_v, wts, layer, state_layer, bsz, seq, nb, lt, c, pos0):
    rows = nb * lt
    tpb = seq // lt
    n_chunks = rows // c
    items = ITEMS_PER_STEP * (SHORT_CHUNK_ITEMS if lt == c else 1)
    gp = min(n_chunks, items // C_HEADS)
    gs = gp if lt == c else 1
    parts = [_pool_parts(pool_buf, wts, layer, state_layer, bsz, seq, nb, lt),
             _delta_parts(conv_buf, s_delta, wts, layer, state_layer, bsz, seq, nb, lt, c),
             _mem_parts(mem_k, mem_v, wts, layer, bsz, seq, nb, lt)]
    gather = lambda key: [item for p in parts for item in p[key]]
    counts = tuple(len(p[key]) for key in ("in_specs", "out_specs", "scratch") for p in parts) + (1,)
    kern = functools.partial(_bcm_kernel, counts=counts, nb=nb, lt=lt, c=c, pos0=pos0, gp=gp, gs=gs)
    ob, pool_n, oc, conv_n, delta_n, om = pl.pallas_call(
        kern,
        grid=(bsz // nb, tpb),
        in_specs=[pl.BlockSpec((rows, D_MODEL), lambda i, t: (i * tpb + t, 0))] + gather("in_specs"),
        out_specs=gather("out_specs"),
        out_shape=gather("out_shape"),
        scratch_shapes=gather("scratch") + [pltpu.VMEM((rows, C_COLS_PAD), F32)],
        compiler_params=pltpu.CompilerParams(
            dimension_semantics=("parallel", "arbitrary"), vmem_limit_bytes=VMEM_LIMIT),
        name="bcm",
    )(x, *gather("args"))
    return ob, pool_n, oc, conv_n, delta_n, om


def _prepare_weights(w_in, rwkv_mu, rwkv_w0, rwkv_w_up, rwkv_a0, rwkv_a_up, rwkv_g_up, rwkv_k_k, rwkv_k_a,
                     rwkv_r_k, rwkv_gn_w, rwkv_gn_b, pool_w, pool_scale, delta_conv_w, delta_a_log,
                     delta_dt_bias, delta_norm_w, w_branch, w_out, ffn1_w_in, ffn1_w_out, ffn2_w_in,
                     ffn2_w_out, ln_g, ln_b):
    depth = w_in.shape[0]
    row = lambda p: p.reshape(depth, 1, -1).astype(F32)
    zeros_lora = jnp.zeros((depth, A_LORA // 2, A_WIDTH), F32)
    lane_row = lambda p: jnp.pad(p.astype(F32), ((0, 0), (C_HEADS, 128 - 2 * C_HEADS))).reshape(depth, 1, 128)
    pool_bd = jnp.zeros((depth, B_WIDTH, B_WIDTH), F32)
    for gidx in range(len(POOL_WINDOWS)):
        sl = slice(gidx * B_GROUP, (gidx + 1) * B_GROUP)
        pool_bd = pool_bd.at[:, sl, sl].set(pool_w[:, gidx].astype(F32))
    head = jnp.arange(A_WIDTH) // A_HEAD
    w_c = jnp.pad(w_in[:, :, OFF_C:OFF_M], ((0, 0), (0, 0), (0, C_COLS_PAD - C_COLS)))
    return {
        "w_a": w_in[:, :, OFF_A:OFF_B].astype(BF16),
        "w_b": w_in[:, :, OFF_B:OFF_C].astype(BF16),
        "w_c": w_c.astype(BF16),
        "w_m": w_in[:, :, OFF_M:OFF_G].astype(BF16),
        "w_g": w_in[:, :, OFF_G:].astype(BF16),
        "rwkv_mu": row(rwkv_mu), "rwkv_w0": row(rwkv_w0), "rwkv_a0": row(rwkv_a0),
        "rwkv_w_up": jnp.concatenate([rwkv_w_up.astype(F32), zeros_lora], axis=1).astype(BF16),
        "rwkv_a_up": jnp.concatenate([zeros_lora, rwkv_a_up.astype(F32)], axis=1).astype(BF16),
        "rwkv_g_up": rwkv_g_up.astype(BF16),
        "rwkv_k_k": row(rwkv_k_k), "rwkv_k_a": row(rwkv_k_a), "rwkv_r_k": row(rwkv_r_k),
        "rwkv_gn_w": row(rwkv_gn_w), "rwkv_gn_b": row(rwkv_gn_b),
        "head_ones": (head[:, None] == head[None, :]).astype(BF16),
        "pool_w": pool_bd.astype(BF16), "pool_scale": row(pool_scale),
        "delta_conv_w": delta_conv_w.astype(F32),
        "delta_a_log": lane_row(delta_a_log), "delta_dt_bias": lane_row(delta_dt_bias),
        "delta_norm_w": row(delta_norm_w),
        "wb_a": w_branch[:, 0:A_WIDTH].astype(BF16),
        "wb_b": w_branch[:, A_WIDTH:A_WIDTH + B_WIDTH].astype(BF16),
        "wb_c": w_branch[:, A_WIDTH + B_WIDTH:A_WIDTH + B_WIDTH + C_WIDTH].astype(BF16),
        "wb_m": w_branch[:, A_WIDTH + B_WIDTH + C_WIDTH:].astype(BF16),
        "w_out": w_out.astype(BF16),
        "ffn1_w_in": ffn1_w_in.astype(BF16), "ffn1_w_out": ffn1_w_out.astype(BF16),
        "ffn2_w_in": ffn2_w_in.astype(BF16), "ffn2_w_out": ffn2_w_out.astype(BF16),
        "ln_g": ln_g.reshape(depth * 3, 1, D_MODEL).astype(F32),
        "ln_b": ln_b.reshape(depth * 3, 1, D_MODEL).astype(F32),
    }


def _trunk(x, mem_k, mem_v, shift, s_rwkv, pool_buf, conv_buf, s_delta, wts, bsz, seq, pos0, stacked,
           tm, nb, nb_bcm, lt, c):
    new = []
    for l in range(DEPTH):
        sl = l if stacked else 0
        x = _ffn(x, wts["ffn1_w_in"], wts["ffn1_w_out"], wts["ln_g"], wts["ln_b"], l, 0, tm)
        oa, shift_n, rwkv_n = _rwkv(x, shift, s_rwkv, wts, l, sl, bsz, seq, nb, lt, c)
        ob, pool_n, oc, conv_n, delta_n, om = _bcm(
            x, pool_buf, conv_buf, s_delta, mem_k, mem_v, wts, l, sl, bsz, seq, nb_bcm, lt, c, pos0)
        x = _merge(x, oa, ob, oc, om, wts, l, tm)
        x = _ffn(x, wts["ffn2_w_in"], wts["ffn2_w_out"], wts["ln_g"], wts["ln_b"], l, 2, tm)
        new.append((rwkv_n, shift_n.reshape(bsz, A_COLS), pool_n, delta_n, conv_n))
    return (x,) + tuple(jnp.stack([st[i] for st in new]) for i in range(5))


def kernel(x_prompt, x_sample, mem_prompt, cache_mem_k, cache_mem_v, state_rwkv, state_rwkv_shift, state_pool, state_delta, state_delta_conv, w_in, rwkv_mu, rwkv_w0, rwkv_w_up, rwkv_a0, rwkv_a_up, rwkv_g_up, rwkv_k_k, rwkv_k_a, rwkv_r_k, rwkv_gn_w, rwkv_gn_b, pool_w, pool_scale, delta_conv_w, delta_a_log, delta_dt_bias, delta_norm_w, mem_w_kv, w_branch, w_out, ffn1_w_in, ffn1_w_out, ffn2_w_in, ffn2_w_out, ln_g, ln_b):
    wts = _prepare_weights(w_in, rwkv_mu, rwkv_w0, rwkv_w_up, rwkv_a0, rwkv_a_up, rwkv_g_up, rwkv_k_k,
                           rwkv_k_a, rwkv_r_k, rwkv_gn_w, rwkv_gn_b, pool_w, pool_scale, delta_conv_w,
                           delta_a_log, delta_dt_bias, delta_norm_w, w_branch, w_out, ffn1_w_in,
                           ffn1_w_out, ffn2_w_in, ffn2_w_out, ln_g, ln_b)
    bp, sp, _ = x_prompt.shape
    bs, ss, _ = x_sample.shape
    n_mem = mem_prompt.shape[1]

    to_t = lambda kv: jnp.transpose(kv, (0, 1, 3, 4, 2)).reshape(DEPTH, kv.shape[1], M_WIDTH, n_mem)
    from_t = lambda kv: jnp.transpose(kv.reshape(DEPTH, kv.shape[1], M_HEADS, M_HEAD, n_mem), (0, 1, 4, 2, 3))
    mem_k, mem_v = _memkv(mem_prompt.reshape(bp * n_mem, D_MODEL),
                          jnp.transpose(mem_w_kv, (0, 2, 1)).astype(BF16), bp, n_mem)

    zeros = lambda *shape: jnp.zeros((1, bp) + shape, F32)
    lt_p = min(sp, 256)
    c_p = min(lt_p, 64)
    prompt = _trunk(
        x_prompt.reshape(bp * sp, D_MODEL), mem_k, mem_v,
        zeros(1, A_COLS), zeros(A_HEADS, A_HEAD, A_HEAD), zeros(POOL_BUF, B_WIDTH),
        zeros(CONV_W - 1, 3 * C_WIDTH), zeros(C_HEADS, C_HEAD, C_HEAD),
        wts, bp, sp, 0, False, tm=min(ROW_TILE, bp * sp), nb=1, nb_bcm=1, lt=lt_p, c=c_p)
    nb_s = min(bs, 32)
    sample = _trunk(
        x_sample.reshape(bs * ss, D_MODEL),
        to_t(cache_mem_k), to_t(cache_mem_v),
        state_rwkv_shift.reshape(DEPTH, bs, 1, A_COLS), state_rwkv, state_pool, state_delta_conv, state_delta,
        wts, bs, ss, PAST_LEN, True, tm=min(ROW_TILE, bs * ss), nb=nb_s, nb_bcm=min(bs, 16), lt=ss, c=ss)

    y_p, p_rwkv, p_shift, p_pool, p_delta, p_conv = prompt
    y_s, s_rwkv, s_shift, s_pool, s_delta, s_conv = sample
    return (y_p.reshape(bp, sp, D_MODEL), y_s.reshape(bs, ss, D_MODEL),
            p_rwkv, p_shift, p_pool, p_delta, p_conv,
            from_t(mem_k), from_t(mem_v),
            s_rwkv, s_shift, s_pool, s_delta, s_conv)
```

```python
import functools
import math

import jax
import jax.numpy as jnp
from jax import lax
from jax.experimental import pallas as pl
from jax.experimental.pallas import tpu as pltpu

F32 = jnp.float32
BF16 = jnp.bfloat16

D_MODEL = 1024
DEPTH = 4
PAST_LEN = 16384
A_WIDTH = 512
A_HEAD = 64
A_HEADS = 8
A_PACK = 4
A_LORA = 128
A_G_RANK = 128
A_COLS = 3 * A_WIDTH + A_LORA + A_G_RANK
GN_EPS = 64e-5
B_WIDTH = 512
B_GROUP = 128
POOL_WINDOWS = (2, 4, 8, 16)
POOL_BUF = 15
C_WIDTH = 512
C_HEAD = 128
C_HEADS = 4
CONV_W = 4
C_COLS = 4 * C_WIDTH + 2 * C_HEADS
C_COLS_PAD = 4 * C_WIDTH + 128
N_MEM = 256
M_HEADS = 4
M_HEAD = 64
M_WIDTH = 256
N_BRANCH = 4
D_FF = 2048
FF_CHUNK = 512
ALPHA = (2.0 * DEPTH) ** 0.25
LN_EPS = 1e-5
EXP_M05 = math.exp(-0.5)
ITEMS_PER_STEP = 16
SHORT_CHUNK_ITEMS = 2
MAX_UNROLLED_STATE_STEPS = 4
MEM_SEQS_PER_STEP = 4
ROW_TILE = 1024
ROW_SLAB = 512

OFF_A = 0
OFF_B = OFF_A + A_COLS
OFF_C = OFF_B + B_WIDTH
OFF_M = OFF_C + C_COLS
OFF_G = OFF_M + M_WIDTH

VMEM_LIMIT = 56 * 1024 * 1024


def _dot(a, b):
    return jnp.dot(a.astype(BF16), b.astype(BF16), preferred_element_type=F32)


def _dot_nt(a, b):
    return lax.dot_general(a.astype(BF16), b.astype(BF16), (((1,), (1,)), ((), ())),
                           preferred_element_type=F32)


def _dot_tn(a, b):
    return lax.dot_general(a.astype(BF16), b.astype(BF16), (((0,), (0,)), ((), ())),
                           preferred_element_type=F32)


def _bf16_pieces(x, n):
    pieces = []
    rest = x
    for _ in range(n):
        p = rest.astype(BF16)
        pieces.append(p)
        rest = rest - p.astype(F32)
    return pieces


def _dot_exact_lhs(m, x, pieces):
    out = None
    for p in _bf16_pieces(x, pieces):
        d = jnp.dot(m, p, preferred_element_type=F32)
        out = d if out is None else out + d
    return out


def _dot_exact_rhs(x, m, pieces):
    out = None
    for p in _bf16_pieces(x, pieces):
        d = jnp.dot(p, m, preferred_element_type=F32)
        out = d if out is None else out + d
    return out


def _layer_norm(z, g, b):
    mu = jnp.mean(z, axis=-1, keepdims=True)
    zc = z - mu
    var = jnp.mean(zc * zc, axis=-1, keepdims=True)
    return zc * lax.rsqrt(var + LN_EPS) * g + b


def _sigmoid(x):
    return 1.0 / (1.0 + jnp.exp(-x))


def _silu(x):
    return x * _sigmoid(x)


def _softplus(x):
    return jnp.maximum(x, 0.0) + jnp.log(1.0 + jnp.exp(-jnp.abs(x)))


def _aligned(index, multiple):
    return index if isinstance(index, int) else pl.multiple_of(index, multiple)


def _tri_masks(c):
    row = lax.broadcasted_iota(jnp.int32, (c, c), 0)
    col = lax.broadcasted_iota(jnp.int32, (c, c), 1)
    return row > col, row >= col, (row == col).astype(F32)


def _segment_cumsum_mask(rows, c):
    shift = c.bit_length() - 1
    row = lax.broadcasted_iota(jnp.int32, (rows, rows), 0)
    col = lax.broadcasted_iota(jnp.int32, (rows, rows), 1)
    same = (row >> shift) == (col >> shift)
    return jnp.where(same & (col <= row), 1.0, 0.0).astype(BF16)


def _tri_inv_many(mats, eye, c, as_rhs=lambda p: p):
    t = [eye + a for a in mats]
    if c <= 2:
        return t
    p = [_dot(a, as_rhs(a)) for a in mats]
    k = 4
    while k < c:
        both = [_dot(jnp.concatenate([p_, t_], axis=0), as_rhs(p_)) for p_, t_ in zip(p, t)]
        t = [t_ + b_[c:] for t_, b_ in zip(t, both)]
        p = [b_[:c] for b_ in both]
        k *= 2
    return [t_ + _dot(t_, as_rhs(p_)) for t_, p_ in zip(t, p)]


def _row_slabs(ref):
    rows = ref.shape[0]
    sub = min(rows, ROW_SLAB)
    return [slice(i * sub, (i + 1) * sub) for i in range(rows // sub)]


def _ffn_kernel(x_ref, win_ref, wout_ref, g_ref, b_ref, o_ref):
    slabs = _row_slabs(x_ref)
    accs = []
    for rs in slabs:
        xb = x_ref[rs, :].astype(BF16)
        acc = None
        for j in range(D_FF // FF_CHUNK):
            lo = j * FF_CHUNK
            gate = jnp.dot(xb, win_ref[:, lo:lo + FF_CHUNK], preferred_element_type=F32)
            up = jnp.dot(xb, win_ref[:, D_FF + lo:D_FF + lo + FF_CHUNK], preferred_element_type=F32)
            part = _dot(_silu(gate) * up, wout_ref[lo:lo + FF_CHUNK, :])
            acc = part if acc is None else acc + part
        accs.append(acc)
    for rs, acc in zip(slabs, accs):
        o_ref[rs, :] = _layer_norm(ALPHA * x_ref[rs, :] + 0.5 * acc, g_ref[...], b_ref[...])


def _swiglu_rows(xb, win_ref, wout_ref):
    acc = None
    for j in range(D_FF // FF_CHUNK):
        lo = j * FF_CHUNK
        gate = jnp.dot(xb, win_ref[:, lo:lo + FF_CHUNK], preferred_element_type=F32)
        up = jnp.dot(xb, win_ref[:, D_FF + lo:D_FF + lo + FF_CHUNK], preferred_element_type=F32)
        part = _dot(_silu(gate) * up, wout_ref[lo:lo + FF_CHUNK, :])
        acc = part if acc is None else acc + part
    return acc


def _ffn_pair_kernel(x_ref, win1_ref, wout1_ref, g1_ref, b1_ref, win2_ref, wout2_ref, g2_ref, b2_ref, o_ref):
    slabs = _row_slabs(x_ref)
    acc1 = [_swiglu_rows(x_ref[rs, :].astype(BF16), win1_ref, wout1_ref) for rs in slabs]
    mid = [_layer_norm(ALPHA * x_ref[rs, :] + 0.5 * acc, g1_ref[...], b1_ref[...]) for rs, acc in zip(slabs, acc1)]
    acc2 = [_swiglu_rows(m.astype(BF16), win2_ref, wout2_ref) for m in mid]
    for rs, m, acc in zip(slabs, mid, acc2):
        o_ref[rs, :] = _layer_norm(ALPHA * m + 0.5 * acc, g2_ref[...], b2_ref[...])


def _ffn_pair(x, wts, layer, tm):
    rows = x.shape[0]
    weights = lambda l: [
        pl.BlockSpec((None, D_MODEL, 2 * D_FF), lambda i: (l, 0, 0), pipeline_mode=pl.Buffered(1)),
        pl.BlockSpec((None, D_FF, D_MODEL), lambda i: (l, 0, 0), pipeline_mode=pl.Buffered(1))]
    norm = lambda row: [pl.BlockSpec((None, 1, D_MODEL), lambda i: (row, 0, 0))] * 2
    return pl.pallas_call(
        _ffn_pair_kernel,
        grid=(rows // tm,),
        in_specs=[pl.BlockSpec((tm, D_MODEL), lambda i: (i, 0))]
        + weights(layer) + norm(layer * 3 + 2) + weights(layer + 1) + norm((layer + 1) * 3),
        out_specs=pl.BlockSpec((tm, D_MODEL), lambda i: (i, 0)),
        out_shape=jax.ShapeDtypeStruct((rows, D_MODEL), F32),
        compiler_params=pltpu.CompilerParams(
            dimension_semantics=("parallel",), vmem_limit_bytes=VMEM_LIMIT),
        name="ffn_pair",
    )(x, wts["ffn2_w_in"], wts["ffn2_w_out"], wts["ln_g"], wts["ln_b"],
      wts["ffn1_w_in"], wts["ffn1_w_out"], wts["ln_g"], wts["ln_b"])


def _ffn(x, w_in, w_out, g, b, layer, idx, tm):
    rows = x.shape[0]
    const = lambda i: (layer, 0, 0)
    return pl.pallas_call(
        _ffn_kernel,
        grid=(rows // tm,),
        in_specs=[
            pl.BlockSpec((tm, D_MODEL), lambda i: (i, 0)),
            pl.BlockSpec((None, D_MODEL, 2 * D_FF), const, pipeline_mode=pl.Buffered(1)),
            pl.BlockSpec((None, D_FF, D_MODEL), const, pipeline_mode=pl.Buffered(1)),
            pl.BlockSpec((None, 1, D_MODEL), lambda i: (layer * 3 + idx, 0, 0)),
            pl.BlockSpec((None, 1, D_MODEL), lambda i: (layer * 3 + idx, 0, 0)),
        ],
        out_specs=pl.BlockSpec((tm, D_MODEL), lambda i: (i, 0)),
        out_shape=jax.ShapeDtypeStruct((rows, D_MODEL), F32),
        compiler_params=pltpu.CompilerParams(
            dimension_semantics=("parallel",), vmem_limit_bytes=VMEM_LIMIT),
        name="ffn",
    )(x, w_in, w_out, g, b)


def _merge_kernel(x_ref, oa_ref, ob_ref, oc_ref, om_ref, wg_ref, wba_ref, wbb_ref, wbc_ref, wbm_ref,
                  wo_ref, g_ref, b_ref, o_ref):
    slabs = _row_slabs(x_ref)
    mixes = []
    for rs in slabs:
        xb = x_ref[rs, :].astype(BF16)
        merged = None
        for i, (br_ref, wb_ref) in enumerate(
                ((oa_ref, wba_ref), (ob_ref, wbb_ref), (oc_ref, wbc_ref), (om_ref, wbm_ref))):
            gate = _sigmoid(jnp.dot(xb, wg_ref[:, i * D_MODEL:(i + 1) * D_MODEL], preferred_element_type=F32))
            term = gate * jnp.dot(br_ref[rs, :], wb_ref[...], preferred_element_type=F32)
            merged = term if merged is None else merged + term
        mixes.append(_dot(merged, wo_ref[...]))
    for rs, mix in zip(slabs, mixes):
        o_ref[rs, :] = _layer_norm(ALPHA * x_ref[rs, :] + mix, g_ref[...], b_ref[...])


def _merge(x, oa, ob, oc, om, wts, layer, tm):
    rows = x.shape[0]
    const = lambda i: (layer, 0, 0)
    row_spec = lambda w: pl.BlockSpec((tm, w), lambda i: (i, 0))
    return pl.pallas_call(
        _merge_kernel,
        grid=(rows // tm,),
        in_specs=[
            row_spec(D_MODEL), row_spec(A_WIDTH), row_spec(B_WIDTH), row_spec(C_WIDTH), row_spec(M_WIDTH),
            pl.BlockSpec((None, D_MODEL, N_BRANCH * D_MODEL), const, pipeline_mode=pl.Buffered(1)),
            pl.BlockSpec((None, A_WIDTH, D_MODEL), const, pipeline_mode=pl.Buffered(1)),
            pl.BlockSpec((None, B_WIDTH, D_MODEL), const, pipeline_mode=pl.Buffered(1)),
            pl.BlockSpec((None, C_WIDTH, D_MODEL), const, pipeline_mode=pl.Buffered(1)),
            pl.BlockSpec((None, M_WIDTH, D_MODEL), const, pipeline_mode=pl.Buffered(1)),
            pl.BlockSpec((None, D_MODEL, D_MODEL), const, pipeline_mode=pl.Buffered(1)),
            pl.BlockSpec((None, 1, D_MODEL), lambda i: (layer * 3 + 1, 0, 0)),
            pl.BlockSpec((None, 1, D_MODEL), lambda i: (layer * 3 + 1, 0, 0)),
        ],
        out_specs=row_spec(D_MODEL),
        out_shape=jax.ShapeDtypeStruct((rows, D_MODEL), F32),
        compiler_params=pltpu.CompilerParams(
            dimension_semantics=("parallel",), vmem_limit_bytes=VMEM_LIMIT),
        name="merge",
    )(x, oa, ob, oc, om, wts["w_g"], wts["wb_a"], wts["wb_b"], wts["wb_c"], wts["wb_m"], wts["w_out"],
      wts["ln_g"], wts["ln_b"])


def _memkv_kernel(m_ref, w_ref, k_ref, v_ref):
    kv = _dot_nt(w_ref[...], m_ref[...])
    k_ref[...] = kv[:M_WIDTH]
    v_ref[...] = kv[M_WIDTH:]


def _memkv(mem, w_kv_t, bsz, n_mem):
    out = jax.ShapeDtypeStruct((DEPTH, bsz, M_WIDTH, n_mem), F32)
    return pl.pallas_call(
        _memkv_kernel,
        grid=(DEPTH, bsz),
        in_specs=[
            pl.BlockSpec((n_mem, D_MODEL), lambda l, b: (b, 0)),
            pl.BlockSpec((None, 2 * M_WIDTH, D_MODEL), lambda l, b: (l, 0, 0)),
        ],
        out_specs=[pl.BlockSpec((None, None, M_WIDTH, n_mem), lambda l, b: (l, b, 0, 0))] * 2,
        out_shape=[out, out],
        compiler_params=pltpu.CompilerParams(
            dimension_semantics=("parallel", "parallel"), vmem_limit_bytes=VMEM_LIMIT),
        name="memkv",
    )(mem, w_kv_t)


def _rwkv_kernel(x_ref, shift_ref, s0_ref, wa_ref, mu_ref, w0_ref, wup_ref, a0_ref, aup_ref, gup_ref,
                 kk_ref, ka_ref, rk_ref, gnw_ref, gnb_ref, e_ref,
                 o_ref, shift_out_ref, s_out_ref,
                 hist_ref, rt_ref, kt_ref, at_ref, bt_ref, v_ref, ec_ref, y_ref, ahat_ref, uv_ref, yv_ref, arb_ref,
                 *, nb, lt, c, gp, gs):
    t = pl.program_id(1)
    rows = nb * lt
    cps = lt // c

    @pl.when(t == 0)
    def _():
        hist_ref[:, 7:8, :] = shift_ref[...]
        s_out_ref[...] = s0_ref[...]

    pa = _dot(x_ref[...], wa_ref[...])
    hist_ref[:, 8:8 + lt, :] = pa.reshape(nb, lt, A_COLS)
    prev = hist_ref[:, 7:7 + lt, :].reshape(rows, A_COLS)
    last = hist_ref[:, 7 + lt:8 + lt, :]
    shift_out_ref[...] = last
    hist_ref[:, 7:8, :] = last

    xm = pa + (prev - pa) * mu_ref[...]
    r = xm[:, 0:A_WIDTH]
    k = xm[:, A_WIDTH:2 * A_WIDTH]
    v = xm[:, 2 * A_WIDTH:3 * A_WIDTH]
    xwa = xm[:, 3 * A_WIDTH:3 * A_WIDTH + A_LORA]
    xg = xm[:, 3 * A_WIDTH + A_LORA:]
    z = w0_ref[...] + _dot(jnp.tanh(xwa), wup_ref[...])
    lw = -EXP_M05 * _sigmoid(z)
    a = _sigmoid(a0_ref[...] + _dot(xwa, aup_ref[...]))
    e = e_ref[...]
    kkr = k * kk_ref[...]
    kk = kkr * lax.rsqrt(_dot_exact_rhs(kkr * kkr, e, 1) + 1e-12)
    k2 = k * (1.0 + (a - 1.0) * ka_ref[...])
    cum = _dot_exact_lhs(_segment_cumsum_mask(rows, c), lw, 2)
    ecum = jnp.exp(cum)
    encum = jnp.exp(-cum)
    rt_ref[...] = r * ecum
    kt_ref[...] = k2 * encum
    bt_ref[...] = kk * a * encum
    at_ref[...] = -kk * jnp.exp(cum - lw)
    v_ref[...] = v
    ec_ref[...] = ecum

    n_chunks = rows // c
    gw = A_PACK * A_HEAD
    xw = A_PACK * c
    groups = [slice(g * gw, (g + 1) * gw) for g in range(A_HEADS // A_PACK)]

    def lane_block_masks(width, block):
        blk = lax.broadcasted_iota(jnp.int32, (c, width), 1) >> (block.bit_length() - 1)
        return [blk == h for h in range(A_PACK)]

    head_masks = lane_block_masks(gw, A_HEAD)
    mat_masks = lane_block_masks(xw, c)

    def block_diag(x, masks):
        return jnp.concatenate([jnp.where(m, x, 0.0) for m in masks], axis=0)

    row1 = lax.broadcasted_iota(jnp.int32, (c, xw), 0)
    col1 = lax.broadcasted_iota(jnp.int32, (c, xw), 1) & (c - 1)
    strict, incl, eye = row1 > col1, row1 >= col1, (row1 == col1).astype(F32)
    row2 = lax.broadcasted_iota(jnp.int32, (2 * c, xw), 0)
    col2 = lax.broadcasted_iota(jnp.int32, (2 * c, xw), 1) & (c - 1)
    strict_incl = jnp.where(row2 < c, row2, row2 - c + 1) > col2

    def state_free(gi, carry):
        pos, rt, kt, at, bt, vv = [], [], [], [], [], []
        for j in range(gp):
            rs = pl.ds(pl.multiple_of((gi * gp + j) * c, c), c)
            tiles = [ref[rs, :] for ref in (rt_ref, kt_ref, at_ref, bt_ref, v_ref)]
            for g, gl in enumerate(groups):
                pos.append((rs, g, gl))
                for dst, tile in zip((rt, kt, at, bt, vv), tiles):
                    dst.append(tile[:, gl])
        ar = [jnp.concatenate([a_, r_], axis=0) for a_, r_ in zip(at, rt)]
        gb = [_dot_nt(x, block_diag(b_, head_masks)) for x, b_ in zip(ar, bt)]
        gk = [_dot_nt(x, block_diag(k_, head_masks)) for x, k_ in zip(ar, kt)]
        a_ab = [jnp.where(strict, x[:c], 0.0) for x in gb]
        tinv = _tri_inv_many(a_ab, eye, c, lambda p: block_diag(p, mat_masks))
        kv = [_dot(jnp.where(strict_incl, x, 0.0), block_diag(v_, head_masks))
              for x, v_ in zip(gk, vv)]
        ahat = [_dot(t_, block_diag(a_, head_masks)) for t_, a_ in zip(tinv, at)]
        uv = [_dot(t_, block_diag(x[:c], head_masks)) for t_, x in zip(tinv, kv)]
        for (rs, g, gl), ah_, uv_, kv_, gb_ in zip(pos, ahat, uv, kv, gb):
            ahat_ref[rs, gl] = ah_
            uv_ref[rs, gl] = uv_
            yv_ref[rs, gl] = kv_[c:]
            arb_ref[rs, g * xw:(g + 1) * xw] = jnp.where(incl, gb_[c:], 0.0)
        return carry

    lax.fori_loop(0, n_chunks // gp, state_free, 0)

    def state_step(gi, carry):
        pos, ahat, rt, uv, yv, arb, vv, bk, wend, st = [], [], [], [], [], [], [], [], [], []
        for j in range(gs):
            i = gi * gs + j
            row0 = _aligned(i * c, c)
            rs = pl.ds(row0, c)
            b = i // cps
            tiles = [ref[rs, :] for ref in (ahat_ref, rt_ref, uv_ref, yv_ref, v_ref, bt_ref, kt_ref)]
            arb_c = arb_ref[rs, :]
            w_end = ec_ref[pl.ds(row0 + c - 1, 1), :]
            for h in range(A_HEADS):
                sl = slice(h * A_HEAD, (h + 1) * A_HEAD)
                pos.append((rs, b, h, sl))
                ahat.append(tiles[0][:, sl])
                rt.append(tiles[1][:, sl])
                uv.append(tiles[2][:, sl])
                yv.append(tiles[3][:, sl])
                vv.append(tiles[4][:, sl])
                bk.append(jnp.concatenate([tiles[5][:, sl], tiles[6][:, sl]], axis=0))
                arb.append(arb_c[:, h * c:(h + 1) * c])
                wend.append(w_end[:, sl])
                st.append(s_out_ref[b, h])
        ps = [_dot_nt(jnp.concatenate([a_, r_], axis=0), s_) for a_, r_, s_ in zip(ahat, rt, st)]
        u = [p_[:c] + x for p_, x in zip(ps, uv)]
        y = [p_[c:] + _dot(m_, u_) + x for p_, m_, u_, x in zip(ps, arb, u, yv)]
        s_new = [(s_ + _dot_tn(jnp.concatenate([u_, v_], axis=0), bk_)) * w_
                 for s_, u_, v_, bk_, w_ in zip(st, u, vv, bk, wend)]
        for (rs, b, h, sl), y_, s_ in zip(pos, y, s_new):
            y_ref[rs, sl] = y_
            s_out_ref[b, h] = s_
        return carry

    late = {}

    def gate():
        late["g"] = _dot(_sigmoid(xg), gup_ref[...])

    def bonus():
        late["bonus"] = _dot_exact_rhs(r * k2 * rk_ref[...], e, 2) * v

    fillers = [gate, bonus]
    n_steps = n_chunks // gs
    if n_steps <= MAX_UNROLLED_STATE_STEPS:
        for i in range(n_steps):
            state_step(i, 0)
            if i < len(fillers):
                fillers[i]()
        for filler in fillers[n_steps:]:
            filler()
    else:
        for filler in fillers:
            filler()
        lax.fori_loop(0, n_steps, state_step, 0)

    y = y_ref[...]
    inv_n = 1.0 / A_HEAD
    mean = _dot_exact_rhs(y, e, 1) * inv_n
    yc = y - mean
    var = _dot_exact_rhs(yc * yc, e, 1) * inv_n
    yn = yc * lax.rsqrt(var + GN_EPS) * gnw_ref[...] + gnb_ref[...]
    o_ref[...] = ((yn + late["bonus"]) * late["g"]).astype(o_ref.dtype)


def _rwkv(x, shift, s0, wts, layer, state_layer, bsz, seq, nb, lt, c):
    rows = nb * lt
    tpb = seq // lt
    n_chunks = rows // c
    gp = min(n_chunks, ITEMS_PER_STEP * A_PACK // (2 * A_HEADS))
    gs = gp if lt == c else 1
    wl = lambda i, t: (layer, 0, 0)
    vec = lambda width: pl.BlockSpec((None, 1, width), wl)
    kern = functools.partial(_rwkv_kernel, nb=nb, lt=lt, c=c, gp=gp, gs=gs)
    return pl.pallas_call(
        kern,
        grid=(bsz // nb, tpb),
        in_specs=[
            pl.BlockSpec((rows, D_MODEL), lambda i, t: (i * tpb + t, 0)),
            pl.BlockSpec((None, nb, 1, A_COLS), lambda i, t: (state_layer, i, 0, 0)),
            pl.BlockSpec((None, nb, A_HEADS, A_HEAD, A_HEAD), lambda i, t: (state_layer, i, 0, 0, 0)),
            pl.BlockSpec((None, D_MODEL, A_COLS), wl),
            vec(A_COLS), vec(A_WIDTH),
            pl.BlockSpec((None, A_LORA, A_WIDTH), wl),
            vec(A_WIDTH),
            pl.BlockSpec((None, A_LORA, A_WIDTH), wl),
            pl.BlockSpec((None, A_G_RANK, A_WIDTH), wl),
            vec(A_WIDTH), vec(A_WIDTH), vec(A_WIDTH), vec(A_WIDTH), vec(A_WIDTH),
            pl.BlockSpec((A_WIDTH, A_WIDTH), lambda i, t: (0, 0)),
        ],
        out_specs=[
            pl.BlockSpec((rows, A_WIDTH), lambda i, t: (i * tpb + t, 0)),
            pl.BlockSpec((nb, 1, A_COLS), lambda i, t: (i, 0, 0)),
            pl.BlockSpec((nb, A_HEADS, A_HEAD, A_HEAD), lambda i, t: (i, 0, 0, 0)),
        ],
        out_shape=[
            jax.ShapeDtypeStruct((bsz * seq, A_WIDTH), BF16),
            jax.ShapeDtypeStruct((bsz, 1, A_COLS), F32),
            jax.ShapeDtypeStruct((bsz, A_HEADS, A_HEAD, A_HEAD), F32),
        ],
        scratch_shapes=[pltpu.VMEM((nb, 8 + lt, A_COLS), F32)]
        + [pltpu.VMEM((rows, A_WIDTH), F32)] * 10
        + [pltpu.VMEM((rows, A_HEADS * c), F32)],
        compiler_params=pltpu.CompilerParams(
            dimension_semantics=("parallel", "arbitrary"), vmem_limit_bytes=VMEM_LIMIT),
        name="rwkv",
    )(x, shift, s0, wts["w_a"], wts["rwkv_mu"], wts["rwkv_w0"], wts["rwkv_w_up"], wts["rwkv_a0"],
      wts["rwkv_a_up"], wts["rwkv_g_up"], wts["rwkv_k_k"], wts["rwkv_k_a"], wts["rwkv_r_k"],
      wts["rwkv_gn_w"], wts["rwkv_gn_b"], wts["head_ones"])


def _pool_init(buf_ref, hist_ref):
    @pl.when(pl.program_id(1) == 0)
    def _():
        hist_ref[:, 1:16, :] = buf_ref[...]


def _pool_kernel(x_ref, buf_ref, wb_ref, pw_ref, ps_ref, o_ref, buf_out_ref, hist_ref, *, nb, lt, pos0):
    t = pl.program_id(1)
    rows = nb * lt
    pb = _dot(x_ref[...], wb_ref[...])
    hist_ref[:, 16:16 + lt, :] = pb.reshape(nb, lt, B_WIDTH)
    pos = pos0 + t * lt + lax.broadcasted_iota(jnp.int32, (1, lt, 1), 1)
    parts = []
    for j, w in enumerate(POOL_WINDOWS):
        cs = slice(j * B_GROUP, (j + 1) * B_GROUP)
        cur = hist_ref[:, 16:16 + lt, cs]
        acc = cur
        for d in range(1, w):
            acc = acc + hist_ref[:, 16 - d:16 - d + lt, cs]
        count = jnp.minimum(pos + 1, w).astype(F32)
        parts.append(acc / count - cur)
    pooled = jnp.concatenate(parts, axis=-1).reshape(rows, B_WIDTH)
    o_ref[...] = (_dot(pooled, pw_ref[...]) * ps_ref[...]).astype(o_ref.dtype)
    tail = hist_ref[:, lt:lt + 16, :]
    buf_out_ref[...] = tail[:, 1:16, :]
    hist_ref[:, 0:16, :] = tail


def _pool_parts(buf, wts, layer, state_layer, bsz, seq, nb, lt):
    rows = nb * lt
    tpb = seq // lt
    wl = lambda i, t: (layer, 0, 0)
    return dict(
        args=[buf, wts["w_b"], wts["pool_w"], wts["pool_scale"]],
        in_specs=[
            pl.BlockSpec((None, nb, POOL_BUF, B_WIDTH), lambda i, t: (state_layer, i, 0, 0)),
            pl.BlockSpec((None, D_MODEL, B_WIDTH), wl),
            pl.BlockSpec((None, B_WIDTH, B_WIDTH), wl),
            pl.BlockSpec((None, 1, B_WIDTH), wl),
        ],
        out_specs=[
            pl.BlockSpec((rows, B_WIDTH), lambda i, t: (i * tpb + t, 0)),
            pl.BlockSpec((nb, POOL_BUF, B_WIDTH), lambda i, t: (i, 0, 0)),
        ],
        out_shape=[
            jax.ShapeDtypeStruct((bsz * seq, B_WIDTH), BF16),
            jax.ShapeDtypeStruct((bsz, POOL_BUF, B_WIDTH), F32),
        ],
        scratch=[pltpu.VMEM((nb, 16 + lt, B_WIDTH), F32)],
    )


def _delta_kernel(pc_ref, conv_ref, s0_ref, wc_ref, cw_ref, alog_ref, dtb_ref, nw_ref,
                  o_ref, conv_out_ref, s_out_ref,
                  hist_ref, q_ref, k_ref, v_ref, beta_ref, cum_ref, y_ref,
                  u_ref, w_ref, qd_ref, kd_ref, aqk_ref, *, nb, lt, c, gp, gs, fillers):
    t = pl.program_id(1)
    rows = nb * lt
    cps = lt // c
    qkv_w = 3 * C_WIDTH

    @pl.when(t == 0)
    def _():
        hist_ref[:, 5:8, :] = conv_ref[...]
        s_out_ref[...] = s0_ref[...]

    hist_ref[:, 8:8 + lt, :] = pc_ref[:, :qkv_w].reshape(nb, lt, qkv_w)
    ba = pc_ref[:, qkv_w + C_WIDTH:]
    conv = None
    for j in range(CONV_W):
        term = hist_ref[:, 5 + j:5 + j + lt, :] * cw_ref[j:j + 1, :]
        conv = term if conv is None else conv + term
    tail = hist_ref[:, 5 + lt:8 + lt, :]
    conv_out_ref[...] = tail
    hist_ref[:, 5:8, :] = tail
    act = _silu(conv).reshape(rows, qkv_w)
    for h in range(C_HEADS):
        qs = slice(h * C_HEAD, (h + 1) * C_HEAD)
        ks = slice(C_WIDTH + h * C_HEAD, C_WIDTH + (h + 1) * C_HEAD)
        qh = act[:, qs]
        kh = act[:, ks]
        q_ref[:, qs] = qh * (lax.rsqrt(jnp.sum(qh * qh, axis=-1, keepdims=True) + 1e-12) * C_HEAD ** -0.5)
        k_ref[:, qs] = kh * lax.rsqrt(jnp.sum(kh * kh, axis=-1, keepdims=True) + 1e-12)
    v_ref[...] = act[:, 2 * C_WIDTH:]
    beta_ref[...] = _sigmoid(ba)
    g = -jnp.exp(alog_ref[...]) * _softplus(ba + dtb_ref[...])
    cum_ref[...] = _dot_exact_lhs(_segment_cumsum_mask(rows, c), g, 3)

    strict, incl, eye = _tri_masks(c)
    n_chunks = rows // c
    heads = [slice(h * C_HEAD, (h + 1) * C_HEAD) for h in range(C_HEADS)]

    def state_free(gi, carry):
        pos, qh, kh, vh, bh, col, rowv, g_end = [], [], [], [], [], [], [], []
        for j in range(gp):
            row0 = pl.multiple_of((gi * gp + j) * c, c)
            rs = pl.ds(row0, c)
            q_c, k_c, v_c = q_ref[rs, :], k_ref[rs, :], v_ref[rs, :]
            beta_c = beta_ref[rs, :]
            cum_c = cum_ref[rs, :]
            cum_t = cum_c.T
            cum_end = cum_ref[pl.ds(row0 + c - 1, 1), :]
            for h, sl in enumerate(heads):
                pos.append((rs, h, sl))
                qh.append(q_c[:, sl])
                kh.append(k_c[:, sl])
                vh.append(v_c[:, sl])
                bh.append(beta_c[:, h:h + 1])
                col.append(cum_c[:, C_HEADS + h:C_HEADS + h + 1])
                rowv.append(cum_t[C_HEADS + h:C_HEADS + h + 1, :])
                g_end.append(cum_end[:, C_HEADS + h:C_HEADS + h + 1])
        decay = [jnp.where(incl, jnp.exp(jnp.where(incl, c_ - r_, 0.0)), 0.0) for c_, r_ in zip(col, rowv)]
        kb = [k_ * b_ for k_, b_ in zip(kh, bh)]
        gm = [_dot_nt(jnp.concatenate([kb_, q_], axis=0), k_) for kb_, q_, k_ in zip(kb, qh, kh)]
        tinv = _tri_inv_many([jnp.where(strict, -(g_[:c] * d_), 0.0) for g_, d_ in zip(gm, decay)], eye, c)
        ecol = [jnp.exp(c_) for c_ in col]
        uw = [_dot(t_, jnp.concatenate([v_ * b_, kb_ * e_], axis=1))
              for t_, v_, b_, kb_, e_ in zip(tinv, vh, bh, kb, ecol)]
        for i, (rs, h, sl) in enumerate(pos):
            u_ref[rs, sl] = uw[i][:, :C_HEAD]
            w_ref[rs, sl] = uw[i][:, C_HEAD:]
            qd_ref[rs, sl] = qh[i] * ecol[i]
            kd_ref[rs, sl] = kh[i] * jnp.exp(g_end[i] - col[i])
            aqk_ref[rs, h * c:(h + 1) * c] = gm[i][c:] * decay[i]
        return carry

    lax.fori_loop(0, n_chunks // gp, state_free, 0)

    def state_step(gi, carry):
        pos, wq, u, aqk, kd, decay_end, st = [], [], [], [], [], [], []
        for j in range(gs):
            i = gi * gs + j
            row0 = _aligned(i * c, c)
            rs = pl.ds(row0, c)
            b = i // cps
            w_c, qd_c, u_c, kd_c, aqk_c = w_ref[rs, :], qd_ref[rs, :], u_ref[rs, :], kd_ref[rs, :], aqk_ref[rs, :]
            e_end = jnp.exp(cum_ref[pl.ds(row0 + c - 1, 1), :])
            for h, sl in enumerate(heads):
                pos.append((rs, b, h, sl))
                wq.append(jnp.concatenate([w_c[:, sl], qd_c[:, sl]], axis=0))
                u.append(u_c[:, sl])
                kd.append(kd_c[:, sl])
                aqk.append(aqk_c[:, h * c:(h + 1) * c])
                decay_end.append(e_end[:, C_HEADS + h:C_HEADS + h + 1])
                st.append(s_out_ref[b, h])
        ps = [_dot(x, s_) for x, s_ in zip(wq, st)]
        v_new = [u_ - p_[:c] for u_, p_ in zip(u, ps)]
        y = [p_[c:] + _dot(a_, v_) for p_, a_, v_ in zip(ps, aqk, v_new)]
        s_new = [s_ * e_ + _dot_tn(k_, v_) for s_, e_, k_, v_ in zip(st, decay_end, kd, v_new)]
        for (rs, b, h, sl), y_, s_ in zip(pos, y, s_new):
            y_ref[rs, sl] = y_
            s_out_ref[b, h] = s_
        return carry

    n_steps = n_chunks // gs
    if n_steps <= MAX_UNROLLED_STATE_STEPS:
        for i in range(n_steps):
            state_step(i, 0)
            if i < len(fillers):
                fillers[i]()
        for filler in fillers[n_steps:]:
            filler()
    else:
        for filler in fillers:
            filler()
        lax.fori_loop(0, n_steps, state_step, 0)

    for h in range(C_HEADS):
        sl = slice(h * C_HEAD, (h + 1) * C_HEAD)
        oh = y_ref[:, sl]
        oh = oh * lax.rsqrt(jnp.mean(oh * oh, axis=-1, keepdims=True) + 1e-6) * nw_ref[...]
        zg = pc_ref[:, qkv_w + h * C_HEAD:qkv_w + (h + 1) * C_HEAD]
        o_ref[:, sl] = (oh * _silu(zg)).astype(o_ref.dtype)


def _delta_parts(conv, s0, wts, layer, state_layer, bsz, seq, nb, lt, c):
    rows = nb * lt
    tpb = seq // lt
    wl = lambda i, t: (layer, 0, 0)
    qkv_w = 3 * C_WIDTH
    return dict(
        args=[conv, s0, wts["w_c"], wts["delta_conv_w"], wts["delta_a_log"], wts["delta_dt_bias"],
              wts["delta_norm_w"]],
        in_specs=[
            pl.BlockSpec((None, nb, CONV_W - 1, qkv_w), lambda i, t: (state_layer, i, 0, 0)),
            pl.BlockSpec((None, nb, C_HEADS, C_HEAD, C_HEAD), lambda i, t: (state_layer, i, 0, 0, 0)),
            pl.BlockSpec((None, D_MODEL, C_COLS_PAD), wl),
            pl.BlockSpec((None, CONV_W, qkv_w), wl),
            pl.BlockSpec((None, 1, 128), wl),
            pl.BlockSpec((None, 1, 128), wl),
            pl.BlockSpec((None, 1, C_HEAD), wl),
        ],
        out_specs=[
            pl.BlockSpec((rows, C_WIDTH), lambda i, t: (i * tpb + t, 0)),
            pl.BlockSpec((nb, CONV_W - 1, qkv_w), lambda i, t: (i, 0, 0)),
            pl.BlockSpec((nb, C_HEADS, C_HEAD, C_HEAD), lambda i, t: (i, 0, 0, 0)),
        ],
        out_shape=[
            jax.ShapeDtypeStruct((bsz * seq, C_WIDTH), BF16),
            jax.ShapeDtypeStruct((bsz, CONV_W - 1, qkv_w), F32),
            jax.ShapeDtypeStruct((bsz, C_HEADS, C_HEAD, C_HEAD), F32),
        ],
        scratch=[pltpu.VMEM((nb, 8 + lt, qkv_w), F32)]
        + [pltpu.VMEM((rows, C_WIDTH), F32)] * 3
        + [pltpu.VMEM((rows, 128), F32)] * 2
        + [pltpu.VMEM((rows, C_WIDTH), F32)] * 5
        + [pltpu.VMEM((rows, C_HEADS * c), F32)],
    )


def _mem_kernel(x_ref, k_ref, v_ref, wm_ref, o_ref, q_ref, *, nb, lt):
    q_ref[...] = _dot(x_ref[...], wm_ref[...]) * (M_HEAD ** -0.5)
    lane = lax.broadcasted_iota(jnp.int32, (1, M_WIDTH), 1)
    heads = [(lane >= h * M_HEAD) & (lane < (h + 1) * M_HEAD) for h in range(M_HEADS)]
    per_step = min(nb, MEM_SEQS_PER_STEP)

    def seqs_body(i, carry):
        seqs = [i * per_step + j for j in range(per_step)]
        rows = [pl.ds(pl.multiple_of(b * lt, lt), lt) for b in seqs]
        qs = [jnp.concatenate([jnp.where(m, q_ref[rs, :], 0.0) for m in heads], axis=0) for rs in rows]
        s = [_dot(q_, k_ref[b]) for q_, b in zip(qs, seqs)]
        p =[jnp.exp(s_ - jnp.max(s_, axis=-1, keepdims=True)) for s_ in s]
        p = [p_ / jnp.sum(p_, axis=-1, keepdims=True) for p_ in p]
        o = [_dot_nt(p_, v_ref[b]) for p_, b in zip(p, seqs)]
        for rs, o_ in zip(rows, o):
            out = jnp.zeros((lt, M_WIDTH), F32)
            for j, m in enumerate(heads):
                out = jnp.where(m, o_[j * lt:(j + 1) * lt], out)
            o_ref[rs, :] = out.astype(o_ref.dtype)
        return carry

    if nb == per_step:
        seqs_body(0, 0)
    else:
        lax.fori_loop(0, nb // per_step, seqs_body, 0)


def _mem_parts(mem_k, mem_v, wts, layer, bsz, seq, nb, lt):
    rows = nb * lt
    tpb = seq // lt
    kv_spec = pl.BlockSpec((None, nb, M_WIDTH, N_MEM), lambda i, t: (layer, i, 0, 0))
    return dict(
        args=[mem_k, mem_v, wts["w_m"]],
        in_specs=[kv_spec, kv_spec, pl.BlockSpec((None, D_MODEL, M_WIDTH), lambda i, t: (layer, 0, 0))],
        out_specs=[pl.BlockSpec((rows, M_WIDTH), lambda i, t: (i * tpb + t, 0))],
        out_shape=[jax.ShapeDtypeStruct((bsz * seq, M_WIDTH), BF16)],
        scratch=[pltpu.VMEM((rows, M_WIDTH), F32)],
    )


def _bcm_kernel(*refs, counts, nb, lt, c, pos0, gp, gs):
    x_ref = refs[0]
    groups = []
    start = 1
    for n in counts:
        groups.append(refs[start:start + n])
        start += n
    pool_in, delta_in, mem_in, pool_out, delta_out, mem_out, pool_scr, delta_scr, mem_scr, (pc_ref,) = groups

    _pool_init(pool_in[0], pool_scr[0])
    pc_ref[...] = _dot(x_ref[...], delta_in[2][...])
    fillers = [
        lambda: _pool_kernel(x_ref, *pool_in, *pool_out, *pool_scr, nb=nb, lt=lt, pos0=pos0),
        lambda: _mem_kernel(x_ref, *mem_in, *mem_out, *mem_scr, nb=nb, lt=lt),
    ]
    _delta_kernel(pc_ref, *delta_in, *delta_out, *delta_scr, nb=nb, lt=lt, c=c, gp=gp, gs=gs,
                  fillers=fillers)


def _bcm(x, pool_buf, conv_buf, s_delta, mem_k, mem_v, wts, layer, state_layer, bsz, seq, nb, lt, c, pos0):
    rows = nb * lt
    tpb = seq // lt
    n_chunks = rows // c
    items = ITEMS_PER_STEP * (SHORT_CHUNK_ITEMS if lt == c else 1)
    gp = min(n_chunks, items // C_HEADS)
    gs = gp if lt == c else 1
    parts = [_pool_parts(pool_buf, wts, layer, state_layer, bsz, seq, nb, lt),
             _delta_parts(conv_buf, s_delta, wts, layer, state_layer, bsz, seq, nb, lt, c),
             _mem_parts(mem_k, mem_v, wts, layer, bsz, seq, nb, lt)]
    gather = lambda key: [item for p in parts for item in p[key]]
    counts = tuple(len(p[key]) for key in ("in_specs", "out_specs", "scratch") for p in parts) + (1,)
    kern = functools.partial(_bcm_kernel, counts=counts, nb=nb, lt=lt, c=c, pos0=pos0, gp=gp, gs=gs)
    ob, pool_n, oc, conv_n, delta_n, om = pl.pallas_call(
        kern,
        grid=(bsz // nb, tpb),
        in_specs=[pl.BlockSpec((rows, D_MODEL), lambda i, t: (i * tpb + t, 0))] + gather("in_specs"),
        out_specs=gather("out_specs"),
        out_shape=gather("out_shape"),
        scratch_shapes=gather("scratch") + [pltpu.VMEM((rows, C_COLS_PAD), F32)],
        compiler_params=pltpu.CompilerParams(
            dimension_semantics=("parallel", "arbitrary"), vmem_limit_bytes=VMEM_LIMIT),
        name="bcm",
    )(x, *gather("args"))
    return ob, pool_n, oc, conv_n, delta_n, om


def _prepare_weights(w_in, rwkv_mu, rwkv_w0, rwkv_w_up, rwkv_a0, rwkv_a_up, rwkv_g_up, rwkv_k_k, rwkv_k_a,
                     rwkv_r_k, rwkv_gn_w, rwkv_gn_b, pool_w, pool_scale, delta_conv_w, delta_a_log,
                     delta_dt_bias, delta_norm_w, w_branch, w_out, ffn1_w_in, ffn1_w_out, ffn2_w_in,
                     ffn2_w_out, ln_g, ln_b):
    depth = w_in.shape[0]
    row = lambda p: p.reshape(depth, 1, -1).astype(F32)
    zeros_lora = jnp.zeros((depth, A_LORA // 2, A_WIDTH), F32)
    lane_row = lambda p: jnp.pad(p.astype(F32), ((0, 0), (C_HEADS, 128 - 2 * C_HEADS))).reshape(depth, 1, 128)
    pool_bd = jnp.zeros((depth, B_WIDTH, B_WIDTH), F32)
    for gidx in range(len(POOL_WINDOWS)):
        sl = slice(gidx * B_GROUP, (gidx + 1) * B_GROUP)
        pool_bd = pool_bd.at[:, sl, sl].set(pool_w[:, gidx].astype(F32))
    head = jnp.arange(A_WIDTH) // A_HEAD
    w_c = jnp.pad(w_in[:, :, OFF_C:OFF_M], ((0, 0), (0, 0), (0, C_COLS_PAD - C_COLS)))
    return {
        "w_a": w_in[:, :, OFF_A:OFF_B].astype(BF16),
        "w_b": w_in[:, :, OFF_B:OFF_C].astype(BF16),
        "w_c": w_c.astype(BF16),
        "w_m": w_in[:, :, OFF_M:OFF_G].astype(BF16),
        "w_g": w_in[:, :, OFF_G:].astype(BF16),
        "rwkv_mu": row(rwkv_mu), "rwkv_w0": row(rwkv_w0), "rwkv_a0": row(rwkv_a0),
        "rwkv_w_up": jnp.concatenate([rwkv_w_up.astype(F32), zeros_lora], axis=1).astype(BF16),
        "rwkv_a_up": jnp.concatenate([zeros_lora, rwkv_a_up.astype(F32)], axis=1).astype(BF16),
        "rwkv_g_up": rwkv_g_up.astype(BF16),
        "rwkv_k_k": row(rwkv_k_k), "rwkv_k_a": row(rwkv_k_a), "rwkv_r_k": row(rwkv_r_k),
        "rwkv_gn_w": row(rwkv_gn_w), "rwkv_gn_b": row(rwkv_gn_b),
        "head_ones": (head[:, None] == head[None, :]).astype(BF16),
        "pool_w": pool_bd.astype(BF16), "pool_scale": row(pool_scale),
        "delta_conv_w": delta_conv_w.astype(F32),
        "delta_a_log": lane_row(delta_a_log), "delta_dt_bias": lane_row(delta_dt_bias),
        "delta_norm_w": row(delta_norm_w),
        "wb_a": w_branch[:, 0:A_WIDTH].astype(BF16),
        "wb_b": w_branch[:, A_WIDTH:A_WIDTH + B_WIDTH].astype(BF16),
        "wb_c": w_branch[:, A_WIDTH + B_WIDTH:A_WIDTH + B_WIDTH + C_WIDTH].astype(BF16),
        "wb_m": w_branch[:, A_WIDTH + B_WIDTH + C_WIDTH:].astype(BF16),
        "w_out": w_out.astype(BF16),
        "ffn1_w_in": ffn1_w_in.astype(BF16), "ffn1_w_out": ffn1_w_out.astype(BF16),
        "ffn2_w_in": ffn2_w_in.astype(BF16), "ffn2_w_out": ffn2_w_out.astype(BF16),
        "ln_g": ln_g.reshape(depth * 3, 1, D_MODEL).astype(F32),
        "ln_b": ln_b.reshape(depth * 3, 1, D_MODEL).astype(F32),
    }


def _trunk(x, mem_k, mem_v, shift, s_rwkv, pool_buf, conv_buf, s_delta, wts, bsz, seq, pos0, stacked,
           tm, nb, nb_bcm, lt, c):
    new = []
    x = _ffn(x, wts["ffn1_w_in"], wts["ffn1_w_out"], wts["ln_g"], wts["ln_b"], 0, 0, tm)
    for l in range(DEPTH):
        sl = l if stacked else 0
        oa, shift_n, rwkv_n = _rwkv(x, shift, s_rwkv, wts, l, sl, bsz, seq, nb, lt, c)
        ob, pool_n, oc, conv_n, delta_n, om = _bcm(
            x, pool_buf, conv_buf, s_delta, mem_k, mem_v, wts, l, sl, bsz, seq, nb_bcm, lt, c, pos0)
        x = _merge(x, oa, ob, oc, om, wts, l, tm)
        if l + 1 < DEPTH:
            x = _ffn_pair(x, wts, l, min(tm, ROW_SLAB))
        else:
            x = _ffn(x, wts["ffn2_w_in"], wts["ffn2_w_out"], wts["ln_g"], wts["ln_b"], l, 2, tm)
        new.append((rwkv_n, shift_n.reshape(bsz, A_COLS), pool_n, delta_n, conv_n))
    return (x,) + tuple(jnp.stack([st[i] for st in new]) for i in range(5))


def kernel(x_prompt, x_sample, mem_prompt, cache_mem_k, cache_mem_v, state_rwkv, state_rwkv_shift, state_pool, state_delta, state_delta_conv, w_in, rwkv_mu, rwkv_w0, rwkv_w_up, rwkv_a0, rwkv_a_up, rwkv_g_up, rwkv_k_k, rwkv_k_a, rwkv_r_k, rwkv_gn_w, rwkv_gn_b, pool_w, pool_scale, delta_conv_w, delta_a_log, delta_dt_bias, delta_norm_w, mem_w_kv, w_branch, w_out, ffn1_w_in, ffn1_w_out, ffn2_w_in, ffn2_w_out, ln_g, ln_b):
    wts = _prepare_weights(w_in, rwkv_mu, rwkv_w0, rwkv_w_up, rwkv_a0, rwkv_a_up, rwkv_g_up, rwkv_k_k,
                           rwkv_k_a, rwkv_r_k, rwkv_gn_w, rwkv_gn_b, pool_w, pool_scale, delta_conv_w,
                           delta_a_log, delta_dt_bias, delta_norm_w, w_branch, w_out, ffn1_w_in,
                           ffn1_w_out, ffn2_w_in, ffn2_w_out, ln_g, ln_b)
    bp, sp, _ = x_prompt.shape
    bs, ss, _ = x_sample.shape
    n_mem = mem_prompt.shape[1]

    to_t = lambda kv: jnp.transpose(kv, (0, 1, 3, 4, 2)).reshape(DEPTH, kv.shape[1], M_WIDTH, n_mem)
    from_t = lambda kv: jnp.transpose(kv.reshape(DEPTH, kv.shape[1], M_HEADS, M_HEAD, n_mem), (0, 1, 4, 2, 3))
    mem_k, mem_v = _memkv(mem_prompt.reshape(bp * n_mem, D_MODEL),
                          jnp.transpose(mem_w_kv, (0, 2, 1)).astype(BF16), bp, n_mem)

    zeros = lambda *shape: jnp.zeros((1, bp) + shape, F32)
    lt_p = min(sp, 256)
    c_p = min(lt_p, 64)
    prompt = _trunk(
        x_prompt.reshape(bp * sp, D_MODEL), mem_k, mem_v,
        zeros(1, A_COLS), zeros(A_HEADS, A_HEAD, A_HEAD), zeros(POOL_BUF, B_WIDTH),
        zeros(CONV_W - 1, 3 * C_WIDTH), zeros(C_HEADS, C_HEAD, C_HEAD),
        wts, bp, sp, 0, False, tm=min(ROW_TILE, bp * sp), nb=1, nb_bcm=1, lt=lt_p, c=c_p)
    nb_s = min(bs, 32)
    sample = _trunk(
        x_sample.reshape(bs * ss, D_MODEL),
        to_t(cache_mem_k), to_t(cache_mem_v),
        state_rwkv_shift.reshape(DEPTH, bs, 1, A_COLS), state_rwkv, state_pool, state_delta_conv, state_delta,
        wts, bs, ss, PAST_LEN, True, tm=min(ROW_TILE, bs * ss), nb=nb_s, nb_bcm=min(bs, 16), lt=ss, c=ss)

    y_p, p_rwkv, p_shift, p_pool, p_delta, p_conv = prompt
    y_s, s_rwkv, s_shift, s_pool, s_delta, s_conv = sample
    return (y_p.reshape(bp, sp, D_MODEL), y_s.reshape(bs, ss, D_MODEL),
            p_rwkv, p_shift, p_pool, p_delta, p_conv,
            from_t(mem_k), from_t(mem_v),
            s_rwkv, s_shift, s_pool, s_delta, s_conv)
```
